```python
import math
import jax
import jax.numpy as jnp
from jax import lax
import numpy as np

D_MODEL = 2048
BATCH = 4
SEQ = 2048
DEPTH = 2
DEC_BATCH = 8
DEC_SEQ = 1
PAST_LEN = 16384
PAGE_SIZE = 128

HEAD_DIM = 128
BRANCH_W = D_MODEL // 4
N_HEADS = BRANCH_W // HEAD_DIM
N_BRANCH = 5
CONV_W = 4
CONV_DIM = 4 * BRANCH_W
CHUNK = 64
QBLOCK = 64
DSA_TOPK = 256
IDX_HEADS = 8
IDX_DIM = 64
LRU_C = 8.0
ROPE_THETA = 10000.0
N_MEM = 256
LN_EPS = 1e-5
NORM_EPS = 1e-6
DEEPNORM_ALPHA = (2 * DEPTH) ** 0.25
DEEPNORM_BETA = (8 * DEPTH) ** -0.25
IN_SPLITS = (CONV_DIM, BRANCH_W, N_HEADS, N_HEADS,
             BRANCH_W, BRANCH_W, BRANCH_W, BRANCH_W, IDX_HEADS * IDX_DIM, IDX_DIM, IDX_HEADS,
             BRANCH_W, BRANCH_W, BRANCH_W, BRANCH_W,
             BRANCH_W,
             BRANCH_W, BRANCH_W,
             N_BRANCH * D_MODEL)
N_IN = sum(IN_SPLITS)

kernel_name = 'hybrid_gdn_dsa_retnet_rglru_step'


def split_in(h):
    offs = [int(o) for o in np.cumsum(IN_SPLITS)[:-1]]
    return jnp.split(h, offs, axis=-1)


def heads(a, d=HEAD_DIM):
    return a.reshape(*a.shape[:-1], a.shape[-1] // d, d)


def layer_norm(x, g, b):
    xf = x.astype(jnp.float32)
    mu = xf.mean(-1, keepdims=True)
    var = jnp.square(xf - mu).mean(-1, keepdims=True)
    y = (xf - mu) * lax.rsqrt(var + LN_EPS) * g.astype(jnp.float32) + b.astype(jnp.float32)
    return y.astype(x.dtype)


def l2norm(x):
    xf = x.astype(jnp.float32)
    return xf * lax.rsqrt(jnp.sum(xf * xf, -1, keepdims=True) + NORM_EPS)


def head_group_norm(o):
    mu = o.mean(-1, keepdims=True)
    var = jnp.square(o - mu).mean(-1, keepdims=True)
    return (o - mu) * lax.rsqrt(var + LN_EPS)


def rope(x, pos):
    half = x.shape[-1] // 2
    inv = ROPE_THETA ** (-jnp.arange(half, dtype=jnp.float32) / half)
    ang = pos.astype(jnp.float32)[:, None] * inv[None, :]
    cos = jnp.cos(ang)[:, None, :]
    sin = jnp.sin(ang)[:, None, :]
    xf = x.astype(jnp.float32)
    x1, x2 = xf[..., :half], xf[..., half:]
    return jnp.concatenate([x1 * cos - x2 * sin, x2 * cos + x1 * sin], -1).astype(x.dtype)


def causal_conv(x, buf, w, b):
    t = x.shape[1]
    xp = jnp.concatenate([buf.astype(x.dtype), x], axis=1)
    y = b + xp[:, 0:t] * w[0]
    for j in range(1, CONV_W):
        y = y + xp[:, j:j + t] * w[j]
    return y, xp[:, t:]


def batch_take(a, idx):
    return jax.vmap(lambda ab, ib: ab[ib])(a, idx)


def to_chunks(a, n, c):
    b, _, h = a.shape[:3]
    a = a.astype(jnp.float32).reshape(b, n, c, h, *a.shape[3:])
    return jnp.moveaxis(a, 3, 2)


def from_chunks(o):
    n, b, h, c, d = o.shape
    return o.transpose(1, 0, 3, 2, 4).reshape(b, n * c, h, d)


def gated_delta_chunked(q, k, v, g, beta, s0):
    t = q.shape[1]
    dv = v.shape[-1]
    c = math.gcd(t, CHUNK)
    n = t // c
    q, k, v, g, beta = (to_chunks(a, n, c) for a in (q, k, v, g, beta))
    gc = jnp.cumsum(g, axis=-1)
    i = jnp.arange(c)
    tril = i[:, None] >= i[None, :]
    strict = i[:, None] > i[None, :]
    decay = jnp.exp(jnp.where(tril, gc[..., :, None] - gc[..., None, :], -jnp.inf))
    kk = jnp.einsum('bnhid,bnhjd->bnhij', k, k)
    m = jnp.eye(c, dtype=jnp.float32) + jnp.where(strict, beta[..., :, None] * kk * decay, 0.0)
    rhs = jnp.concatenate([v * beta[..., None], k * (beta * jnp.exp(gc))[..., None]], axis=-1)
    sol = lax.linalg.triangular_solve(m, rhs, left_side=True, lower=True, unit_diagonal=True)
    u, w = sol[..., :dv], sol[..., dv:]
    qk = jnp.einsum('bnhid,bnhjd->bnhij', q, k) * decay
    q_dec = q * jnp.exp(gc)[..., None]
    k_tail = k * jnp.exp(gc[..., -1:] - gc)[..., None]
    g_last = jnp.exp(gc[..., -1])

    def step(s, xs):
        u_n, w_n, qd_n, qk_n, kt_n, gl_n = xs
        v_new = u_n - jnp.einsum('bhcd,bhde->bhce', w_n, s)
        o = jnp.einsum('bhcd,bhde->bhce', qd_n, s) + jnp.einsum('bhij,bhje->bhie', qk_n, v_new)
        s = s * gl_n[..., None, None] + jnp.einsum('bhcd,bhce->bhde', kt_n, v_new)
        return s, o

    xs = tuple(jnp.moveaxis(a, 1, 0) for a in (u, w, q_dec, qk, k_tail, g_last))
    s_fin, o = lax.scan(step, s0.astype(jnp.float32), xs)
    return from_chunks(o), s_fin


def retention_chunked(q, k, v, r0):
    t, h = q.shape[1], q.shape[2]
    c = math.gcd(t, CHUNK)
    n = t // c
    lg = jnp.log(1.0 - 2.0 ** (-5.0 - jnp.arange(h, dtype=jnp.float32)))
    q, k, v = (to_chunks(a, n, c) for a in (q, k, v))
    i = jnp.arange(c, dtype=jnp.float32)
    tril = i[:, None] >= i[None, :]
    dist = jnp.where(tril, i[:, None] - i[None, :], 0.0)
    dmat = jnp.where(tril, jnp.exp(dist[None] * lg[:, None, None]), 0.0)
    o_in = jnp.einsum('bnhij,bnhje->bnhie', jnp.einsum('bnhid,bnhjd->bnhij', q, k) * dmat, v)
    xi = jnp.exp((i + 1.0)[None, :] * lg[:, None])
    zeta = jnp.exp((c - 1.0 - i)[None, :] * lg[:, None])
    g_c = jnp.exp(c * lg)
    q_x = q * xi[..., None]
    k_z = k * zeta[..., None]

    def step(r, xs):
        qx_n, kz_n, v_n = xs
        o = jnp.einsum('bhcd,bhde->bhce', qx_n, r)
        r = r * g_c[None, :, None, None] + jnp.einsum('bhcd,bhce->bhde', kz_n, v_n)
        return r, o

    xs = tuple(jnp.moveaxis(a, 1, 0) for a in (q_x, k_z, v))
    r_fin, o_cross = lax.scan(step, r0.astype(jnp.float32), xs)
    o = jnp.moveaxis(o_in, 1, 0) + o_cross
    return from_chunks(o), r_fin


def rglru(x, r, i, lam, h0):
    x, r, i = (a.astype(jnp.float32) for a in (x, r, i))
    log_a = -LRU_C * r * jax.nn.softplus(-lam.astype(jnp.float32))
    a = jnp.exp(log_a)
    b = jnp.sqrt(-jnp.expm1(2.0 * log_a)) * (i * x)

    def step(h, ab):
        h = ab[0] * h + ab[1]
        return h, h

    h_fin, hs = lax.scan(step, h0.astype(jnp.float32), (a.swapaxes(0, 1), b.swapaxes(0, 1)))
    return hs.swapaxes(0, 1), h_fin


def dsa_sweep(q, qi, wi, q_pos, ki_all, gather, ksel):
    bsz, t = q.shape[:2]
    qb = math.gcd(t, QBLOCK)
    nb = t // qb
    key_pos = jnp.arange(ki_all.shape[1])
    scale = HEAD_DIM ** -0.5

    def blocks(a):
        return a.reshape(bsz, nb, qb, *a.shape[2:]).swapaxes(0, 1)

    def one_block(args):
        q_blk, qi_blk, w_blk, p_blk = args
        s = jnp.einsum('bqhd,bsd->bqhs', qi_blk, ki_all).astype(jnp.float32)
        score = jnp.einsum('bqhs,bqh->bqs', jax.nn.relu(s), w_blk.astype(jnp.float32))
        score = jnp.where(key_pos[None, None, :] <= p_blk[None, :, None], score, -jnp.inf)
        _, idx = lax.top_k(score, ksel)
        valid = idx <= p_blk[None, :, None]
        k_g, v_g = gather(idx)
        logits = jnp.einsum('bqhd,bqkhd->bqhk', q_blk, k_g).astype(jnp.float32) * scale
        logits = jnp.where(valid[:, :, None, :], logits, -jnp.inf)
        prob = jax.nn.softmax(logits, axis=-1).astype(v_g.dtype)
        return jnp.einsum('bqhk,bqkhd->bqhd', prob, v_g)

    out = lax.map(one_block, (blocks(q), blocks(qi), blocks(wi), q_pos.reshape(nb, qb)))
    return out.swapaxes(0, 1).reshape(bsz, t, *out.shape[3:])


def dsa_prompt(q, k, v, qi, ki, wi, pos):
    ksel = min(DSA_TOPK, q.shape[1] // 4)
    gather = lambda idx: (batch_take(k, idx), batch_take(v, idx))
    return dsa_sweep(q, qi, wi, pos, ki, gather, ksel)


def make_dsa_sample(pool_k, pool_v, pool_ik, page_table):
    def attend(q, k, v, qi, ki, wi, pos):
        db, t = q.shape[:2]
        past = page_table.shape[1] * PAGE_SIZE
        ik_past = pool_ik[page_table].reshape(db, past, IDX_DIM)
        ki_all = jnp.concatenate([ik_past.astype(ki.dtype), ki], axis=1)
        ksel = min(DSA_TOPK, (past + t) // 4)

        def gather(idx):
            in_past = idx < past
            pi = jnp.minimum(idx, past - 1)
            phys = jnp.take_along_axis(page_table, (pi // PAGE_SIZE).reshape(db, -1), axis=1).reshape(idx.shape)
            off = pi % PAGE_SIZE
            ni = jnp.clip(idx - past, 0, t - 1)
            sel = in_past[..., None, None]
            k_g = jnp.where(sel, pool_k[phys, off].astype(k.dtype), batch_take(k, ni))
            v_g = jnp.where(sel, pool_v[phys, off].astype(v.dtype), batch_take(v, ni))
            return k_g, v_g

        return dsa_sweep(q, qi, wi, pos, ki_all, gather, ksel)
    return attend


def mem_attend(q, mk, mv):
    logits = jnp.einsum('bthd,bmhd->bhtm', q, mk).astype(jnp.float32) * HEAD_DIM ** -0.5
    prob = jax.nn.softmax(logits, axis=-1).astype(mv.dtype)
    return jnp.einsum('bhtm,bmhd->bthd', prob, mv)


def mixer_layer(x, pos0, conv_buf, s_gdn, s_ret, h_lru, mem_k, mem_v, dsa_attend, p):
    bsz, t = x.shape[:2]
    dt = x.dtype
    pos = pos0 + jnp.arange(t)
    (conv_in, gdn_z, gdn_a, gdn_b, dsa_q, dsa_k, dsa_v, dsa_z, idx_q, idx_k, idx_w,
     ret_q, ret_k, ret_v, ret_z, lru_z, mem_q, mem_z, gate_pre) = split_in(x @ p['w_in'])

    conv_out, new_conv = causal_conv(conv_in, conv_buf, p['conv_w'], p['conv_b'])
    gq, gk, gv = jnp.split(jax.nn.silu(conv_out[..., :3 * BRANCH_W]), 3, axis=-1)
    lru_x = conv_out[..., 3 * BRANCH_W:]

    q_a = l2norm(heads(gq)) * HEAD_DIM ** -0.5
    k_a = l2norm(heads(gk))
    beta = jax.nn.sigmoid(gdn_b.astype(jnp.float32))
    g = -jnp.exp(p['gdn_a_log'].astype(jnp.float32)) * jax.nn.softplus(gdn_a.astype(jnp.float32) + p['gdn_dt_bias'])
    o_a, new_gdn = gated_delta_chunked(q_a, k_a, heads(gv), g, beta, s_gdn)
    o_a = o_a * lax.rsqrt(jnp.mean(o_a * o_a, -1, keepdims=True) + NORM_EPS) * p['gdn_norm_w']
    y_a = o_a.reshape(bsz, t, BRANCH_W).astype(dt) * jax.nn.silu(gdn_z)

    q_b = rope(heads(dsa_q), pos)
    k_b = rope(heads(dsa_k), pos)
    v_b = heads(dsa_v)
    qi = rope(heads(idx_q, IDX_DIM), pos)
    ki = rope(idx_k[:, :, None, :], pos)[:, :, 0]
    wi = idx_w * (IDX_HEADS * IDX_DIM) ** -0.5
    o_b = dsa_attend(q_b, k_b, v_b, qi, ki, wi, pos)
    y_b = o_b.reshape(bsz, t, BRANCH_W) * jax.nn.silu(dsa_z)

    q_c = rope(heads(ret_q), pos)
    k_c = rope(heads(ret_k), pos) * HEAD_DIM ** -0.5
    o_c, new_ret = retention_chunked(q_c, k_c, heads(ret_v), s_ret)
    y_c = head_group_norm(o_c).reshape(bsz, t, BRANCH_W).astype(dt) * jax.nn.silu(ret_z)

    xb = heads(lru_x)
    r = jax.nn.sigmoid(jnp.einsum('bthi,hij->bthj', xb, p['lru_w_r']).reshape(bsz, t, BRANCH_W) + p['lru_b_r'])
    ig = jax.nn.sigmoid(jnp.einsum('bthi,hij->bthj', xb, p['lru_w_i']).reshape(bsz, t, BRANCH_W) + p['lru_b_i'])
    o_d, new_h = rglru(lru_x, r, ig, p['lru_lambda'], h_lru)
    y_d = o_d.astype(dt) * jax.nn.silu(lru_z)

    y_e = mem_attend(heads(mem_q), mem_k, mem_v).reshape(bsz, t, BRANCH_W) * jax.nn.silu(mem_z)

    ys = (y_a, y_b, y_c, y_d, y_e)
    gates = jax.nn.sigmoid(gate_pre.reshape(bsz, t, N_BRANCH, D_MODEL))
    w_br = p['w_branch']
    merged = gates[:, :, 0] * (ys[0] @ w_br[0])
    for nbr in range(1, N_BRANCH):
        merged = merged + gates[:, :, nbr] * (ys[nbr] @ w_br[nbr])
    out = merged @ p['w_out']
    y = layer_norm(DEEPNORM_ALPHA * x + out, p['ln_g'], p['ln_b'])
    states = (new_conv, new_gdn.astype(dt), new_ret.astype(dt), new_h.astype(dt))
    rows = (k_b, v_b, ki)
    return y, states, rows


def setup_inputs(seed: int = 0) -> dict:
    key = jax.random.key(seed)
    ks = iter(jax.random.split(key, 40))
    f32 = jnp.float32
    n_pages = PAST_LEN // PAGE_SIZE
    used = DEC_BATCH * n_pages
    n_pool = used + max(1, used // 4)

    def normal(shape, scale=1.0):
        return scale * jax.random.normal(next(ks), shape, f32)

    x_prompt = normal((BATCH, SEQ, D_MODEL))
    x_sample = normal((DEC_BATCH, DEC_SEQ, D_MODEL))
    cache_conv = normal((DEPTH, DEC_BATCH, CONV_W - 1, CONV_DIM))
    cache_dsa_k = normal((DEPTH, n_pool, PAGE_SIZE, N_HEADS, HEAD_DIM))
    cache_dsa_v = normal((DEPTH, n_pool, PAGE_SIZE, N_HEADS, HEAD_DIM))
    cache_dsa_idx_k = normal((DEPTH, n_pool, PAGE_SIZE, IDX_DIM))
    cache_mem_k = normal((DEPTH, DEC_BATCH, N_MEM, N_HEADS, HEAD_DIM))
    cache_mem_v = normal((DEPTH, DEC_BATCH, N_MEM, N_HEADS, HEAD_DIM))
    state_gdn = normal((DEPTH, DEC_BATCH, N_HEADS, HEAD_DIM, HEAD_DIM), 0.1)
    state_ret = normal((DEPTH, DEC_BATCH, N_HEADS, HEAD_DIM, HEAD_DIM), 0.5)
    state_lru = normal((DEPTH, DEC_BATCH, BRANCH_W))
    page_table = jax.random.permutation(next(ks), n_pool)[:used].reshape(DEC_BATCH, n_pages).astype(jnp.int32)
    mem_prompt = normal((BATCH, N_MEM, D_MODEL))

    w_in = normal((DEPTH, D_MODEL, N_IN), D_MODEL ** -0.5)
    conv_w = normal((DEPTH, CONV_W, CONV_DIM), 0.5)
    conv_b = normal((DEPTH, CONV_DIM), 0.01)
    a_init = jax.random.uniform(next(ks), (DEPTH, N_HEADS), f32, 1.0, 16.0)
    gdn_a_log = jnp.log(a_init)
    dt_init = jnp.exp(jax.random.uniform(next(ks), (DEPTH, N_HEADS), f32, math.log(1e-3), math.log(1e-1)))
    gdn_dt_bias = dt_init + jnp.log(-jnp.expm1(-dt_init))
    gdn_norm_w = 1.0 + normal((DEPTH, HEAD_DIM), 0.01)
    a0 = jax.random.uniform(next(ks), (DEPTH, BRANCH_W), f32, 0.9, 0.999)
    s0 = a0 ** (1.0 / LRU_C)
    lru_lambda = jnp.log(s0) - jnp.log1p(-s0)
    lru_w_r = normal((DEPTH, N_HEADS, HEAD_DIM, HEAD_DIM), HEAD_DIM ** -0.5)
    lru_b_r = normal((DEPTH, BRANCH_W), 0.01)
    lru_w_i = normal((DEPTH, N_HEADS, HEAD_DIM, HEAD_DIM), HEAD_DIM ** -0.5)
    lru_b_i = normal((DEPTH, BRANCH_W), 0.01)
    w_mem_kv = normal((DEPTH, D_MODEL, 2 * BRANCH_W), D_MODEL ** -0.5)
    w_branch = normal((DEPTH, N_BRANCH, BRANCH_W, D_MODEL), BRANCH_W ** -0.5 * DEEPNORM_BETA)
    w_out = normal((DEPTH, D_MODEL, D_MODEL), D_MODEL ** -0.5 * DEEPNORM_BETA)
    ln_g = 1.0 + normal((DEPTH, D_MODEL), 0.01)
    ln_b = normal((DEPTH, D_MODEL), 0.01)
    return {'x_prompt': x_prompt, 'x_sample': x_sample, 'cache_conv': cache_conv,
            'cache_dsa_k': cache_dsa_k, 'cache_dsa_v': cache_dsa_v, 'cache_dsa_idx_k': cache_dsa_idx_k,
            'cache_mem_k': cache_mem_k, 'cache_mem_v': cache_mem_v, 'state_gdn': state_gdn,
            'state_ret': state_ret, 'state_lru': state_lru, 'page_table': page_table,
            'mem_prompt': mem_prompt, 'w_in': w_in, 'conv_w': conv_w, 'conv_b': conv_b,
            'gdn_a_log': gdn_a_log, 'gdn_dt_bias': gdn_dt_bias, 'gdn_norm_w': gdn_norm_w,
            'lru_lambda': lru_lambda, 'lru_w_r': lru_w_r, 'lru_b_r': lru_b_r, 'lru_w_i': lru_w_i,
            'lru_b_i': lru_b_i, 'w_mem_kv': w_mem_kv, 'w_branch': w_branch, 'w_out': w_out,
            'ln_g': ln_g, 'ln_b': ln_b}


def reference(x_prompt, x_sample, cache_conv, cache_dsa_k, cache_dsa_v, cache_dsa_idx_k,
              cache_mem_k, cache_mem_v, state_gdn, state_ret, state_lru, page_table, mem_prompt,
              w_in, conv_w, conv_b, gdn_a_log, gdn_dt_bias, gdn_norm_w, lru_lambda, lru_w_r,
              lru_b_r, lru_w_i, lru_b_i, w_mem_kv, w_branch, w_out, ln_g, ln_b):
    bsz = x_prompt.shape[0]
    dt = x_prompt.dtype
    past = page_table.shape[1] * PAGE_SIZE
    xp, xs = x_prompt, x_sample
    names = ('conv', 'gdn', 'ret', 'lru', 'dsa_k', 'dsa_v', 'dsa_ik')
    acc_p = {nm: [] for nm in names + ('mem_k', 'mem_v')}
    acc_s = {nm: [] for nm in names}
    for l in range(DEPTH):
        p = {'w_in': w_in[l], 'conv_w': conv_w[l], 'conv_b': conv_b[l], 'gdn_a_log': gdn_a_log[l],
             'gdn_dt_bias': gdn_dt_bias[l], 'gdn_norm_w': gdn_norm_w[l], 'lru_lambda': lru_lambda[l],
             'lru_w_r': lru_w_r[l], 'lru_b_r': lru_b_r[l], 'lru_w_i': lru_w_i[l], 'lru_b_i': lru_b_i[l],
             'w_branch': w_branch[l], 'w_out': w_out[l], 'ln_g': ln_g[l], 'ln_b': ln_b[l]}
        mkv = mem_prompt @ w_mem_kv[l]
        mk, mv = heads(mkv[..., :BRANCH_W]), heads(mkv[..., BRANCH_W:])
        xp, st, rows = mixer_layer(
            xp, 0,
            jnp.zeros((bsz, CONV_W - 1, CONV_DIM), dt),
            jnp.zeros((bsz, N_HEADS, HEAD_DIM, HEAD_DIM), jnp.float32),
            jnp.zeros((bsz, N_HEADS, HEAD_DIM, HEAD_DIM), jnp.float32),
            jnp.zeros((bsz, BRANCH_W), jnp.float32),
            mk, mv, dsa_prompt, p)
        for nm, a in zip(names, st + rows):
            acc_p[nm].append(a)
        acc_p['mem_k'].append(mk)
        acc_p['mem_v'].append(mv)
        dsa_s = make_dsa_sample(cache_dsa_k[l], cache_dsa_v[l], cache_dsa_idx_k[l], page_table)
        xs, st, rows = mixer_layer(
            xs, past, cache_conv[l], state_gdn[l], state_ret[l], state_lru[l],
            cache_mem_k[l], cache_mem_v[l], dsa_s, p)
        for nm, a in zip(names, st + rows):
            acc_s[nm].append(a)
    return (xp, xs,
            jnp.stack(acc_p['conv']), jnp.stack(acc_p['gdn']), jnp.stack(acc_p['ret']), jnp.stack(acc_p['lru']),
            jnp.stack(acc_p['dsa_k']), jnp.stack(acc_p['dsa_v']), jnp.stack(acc_p['dsa_ik']),
            jnp.stack(acc_p['mem_k']), jnp.stack(acc_p['mem_v']),
            jnp.stack(acc_s['conv']), jnp.stack(acc_s['gdn']), jnp.stack(acc_s['ret']), jnp.stack(acc_s['lru']),
            jnp.stack(acc_s['dsa_k']), jnp.stack(acc_s['dsa_v']), jnp.stack(acc_s['dsa_ik']))
```

```python
import functools
import math

import numpy as np
import jax
import jax.numpy as jnp
from jax import lax
from jax.experimental import pallas as pl
from jax.experimental.pallas import tpu as pltpu

F32 = jnp.float32
BF16 = jnp.bfloat16
HI = lax.Precision.HIGHEST

D_MODEL = 2048
HEAD_DIM = 128
BRANCH_W = 512
N_HEADS = 4
N_BRANCH = 5
CONV_W = 4
CONV_DIM = 2048
CHUNK = 64
DSA_TOPK = 256
IDX_HEADS = 8
IDX_DIM = 64
LRU_C = 8.0
ROPE_THETA = 10000.0
PAGE_SIZE = 128
LN_EPS = 1e-5
NORM_EPS = 1e-6
DEPTH = 2
DEEPNORM_ALPHA = (2 * DEPTH) ** 0.25
IN_SPLITS = (CONV_DIM, BRANCH_W, N_HEADS, N_HEADS,
             BRANCH_W, BRANCH_W, BRANCH_W, BRANCH_W, IDX_HEADS * IDX_DIM, IDX_DIM, IDX_HEADS,
             BRANCH_W, BRANCH_W, BRANCH_W, BRANCH_W,
             BRANCH_W,
             BRANCH_W, BRANCH_W,
             N_BRANCH * D_MODEL)

LANE = 128
C_CONV = 0
C_GDNZ = 2048
C_DSA = 2560
C_RET = 4608
C_LRUZ = 6656
C_MEM = 7168
C_QI = 8192
C_KI = 9216
C_SMALL = 9344
N_MAIN = 9728
N_GATE = N_BRANCH * D_MODEL
SCALE = HEAD_DIM ** -0.5
IDX_SCALE = (IDX_HEADS * IDX_DIM) ** -0.5
NEG = -1e30
VMEM_LIMIT = 56 * 1024 * 1024


def _cp(n_grid):
    return pltpu.CompilerParams(dimension_semantics=("arbitrary",) * n_grid, vmem_limit_bytes=VMEM_LIMIT)


def _b(x):
    return x.astype(BF16)


def _mm(a, b, precision=None):
    return jnp.dot(a, b, precision=precision, preferred_element_type=F32)


def _nt(a, b, precision=None):
    return lax.dot_general(a, b, (((1,), (1,)), ((), ())), precision=precision, preferred_element_type=F32)


def _tn(a, b, precision=None):
    return lax.dot_general(a, b, (((0,), (0,)), ((), ())), precision=precision, preferred_element_type=F32)


def _silu(x):
    return x * jax.nn.sigmoid(x)


def _rope128(x, cos, sin):
    return x * cos + pltpu.roll(x, 64, 1) * sin


def _rope_idx(x, cos, sina, sinb):
    return x * cos + pltpu.roll(x, 96, 1) * sina + pltpu.roll(x, 32, 1) * sinb


def _rope_tables(pos):
    pos = np.asarray(pos, np.float64)[:, None]
    ang = pos * (ROPE_THETA ** (-np.arange(64, dtype=np.float64) / 64))[None, :]
    c, s = np.cos(ang), np.sin(ang)
    cos128 = np.concatenate([c, c], 1)
    sin128 = np.concatenate([-s, s], 1)
    ang32 = pos * (ROPE_THETA ** (-np.arange(32, dtype=np.float64) / 32))[None, :]
    c, s = np.cos(ang32), np.sin(ang32)
    z = np.zeros_like(c)
    cos_i = np.concatenate([c, c, z, z], 1)
    sin_ia = np.concatenate([-s, z, z, z], 1)
    sin_ib = np.concatenate([z, s, z, z], 1)
    return tuple(jnp.asarray(a, F32) for a in (cos128, sin128, cos_i, sin_ia, sin_ib))


def _mm_kernel(x_ref, w_ref, o_ref):
    o_ref[...] = _mm(x_ref[...], w_ref[...]).astype(o_ref.dtype)


def _matmul(x, w, out_dtype, tm, tn):
    m, k = x.shape
    n = w.shape[1]
    return pl.pallas_call(
        _mm_kernel,
        grid=(m // tm, n // tn),
        in_specs=[pl.BlockSpec((tm, k), lambda i, j: (i, 0)),
                  pl.BlockSpec((k, tn), lambda i, j: (0, j))],
        out_specs=pl.BlockSpec((tm, tn), lambda i, j: (i, j)),
        out_shape=jax.ShapeDtypeStruct((m, n), out_dtype),
        compiler_params=_cp(2),
        name="proj_matmul",
    )(x, w)


def _prep_w_in(w):
    offs = np.concatenate([[0], np.cumsum(IN_SPLITS)]).astype(int)
    seg = [w[:, offs[i]:offs[i + 1]] for i in range(len(IN_SPLITS))]
    (conv_in, gdn_z, gdn_a, gdn_b, dsa_q, dsa_k, dsa_v, dsa_z, idx_q, idx_k, idx_w,
     ret_q, ret_k, ret_v, ret_z, lru_z, mem_q, mem_z, gate_pre) = seg
    d = w.shape[0]
    zeros = lambda n: jnp.zeros((d, n), w.dtype)
    qi = jnp.concatenate([idx_q.reshape(d, IDX_HEADS, IDX_DIM), jnp.zeros((d, IDX_HEADS, LANE - IDX_DIM), w.dtype)], -1)
    qi = qi.reshape(d, IDX_HEADS * LANE)
    ki = jnp.concatenate([idx_k, zeros(LANE - IDX_DIM)], 1)
    small = jnp.concatenate([gdn_a, gdn_b, idx_w, zeros(LANE - 16)], 1)
    main = jnp.concatenate([conv_in, gdn_z, dsa_q, dsa_k, dsa_v, dsa_z, ret_q, ret_k, ret_v, ret_z,
                            lru_z, mem_q, mem_z, qi, ki, small, zeros(N_MAIN - C_SMALL - LANE)], 1)
    return main.astype(BF16), gate_pre.astype(BF16)


def _ropek_kernel(k_ref, v_ref, ki_ref, cos_ref, sin_ref, ci_ref, sia_ref, sib_ref,
                  kb_ref, k16_ref, v16_ref, kif_ref, ki16_ref):
    cos, sin = cos_ref[...], sin_ref[...]
    for h in range(N_HEADS):
        sl = slice(h * HEAD_DIM, (h + 1) * HEAD_DIM)
        r = _rope128(k_ref[:, sl], cos, sin)
        kb_ref[:, sl] = r
        k16_ref[:, sl] = _b(r)
    v16_ref[...] = _b(v_ref[...])
    ki = _rope_idx(ki_ref[...], ci_ref[...], sia_ref[...], sib_ref[...])
    kif_ref[...] = ki
    ki16_ref[...] = _b(ki)


def _rope_k(h3, tabs, tr):
    bsz, t, _ = h3.shape
    cb = lambda c, w: c // w
    tab = pl.BlockSpec((tr, LANE), lambda b, i: (i, 0))
    row = lambda w, c: pl.BlockSpec((None, tr, w), lambda b, i: (b, i, c))
    return pl.pallas_call(
        _ropek_kernel,
        grid=(bsz, t // tr),
        in_specs=[row(512, cb(C_DSA + 512, 512)), row(512, cb(C_DSA + 1024, 512)), row(LANE, cb(C_KI, LANE)),
                  tab, tab, tab, tab, tab],
        out_specs=[row(512, 0), row(512, 0), row(512, 0), row(LANE, 0), row(LANE, 0)],
        out_shape=[jax.ShapeDtypeStruct((bsz, t, 512), F32), jax.ShapeDtypeStruct((bsz, t, 512), BF16),
                   jax.ShapeDtypeStruct((bsz, t, 512), BF16), jax.ShapeDtypeStruct((bsz, t, LANE), F32),
                   jax.ShapeDtypeStruct((bsz, t, LANE), BF16)],
        compiler_params=_cp(2),
        name="rope_k",
    )(h3, h3, h3, *tabs)


def _gdn_kernel(q_ref, k_ref, v_ref, z_ref, sm_ref, wq_ref, wk_ref, wv_ref, bq_ref, bk_ref, bv_ref,
                hq_ref, hk_ref, hv_ref, par_ref, s0_ref, y_ref, sfin_ref,
                scq, sck, scv, u_s, w_s, qk_s, qd_s, kt_s, gl_s, *, t, c):
    h = pl.program_id(1)
    nc = t // c
    for src, hist, scr in ((q_ref, hq_ref, scq), (k_ref, hk_ref, sck), (v_ref, hv_ref, scv)):
        scr[pl.ds(0, 8), :] = hist[...]
        scr[pl.ds(8, t), :] = src[...]
    ii = lax.broadcasted_iota(jnp.int32, (c, c), 0)
    jj = lax.broadcasted_iota(jnp.int32, (c, c), 1)
    tril = ii >= jj
    strict = ii > jj
    eye = jnp.where(ii == jj, 1.0, 0.0).astype(F32)
    trilf = jnp.where(tril, 1.0, 0.0).astype(F32)
    lane = lax.broadcasted_iota(jnp.int32, (c, LANE), 1)
    e0 = jnp.where(lane == 0, 1.0, 0.0).astype(F32)
    neg_a = -jnp.exp(par_ref[0:1, 0:1])
    dt_bias = par_ref[1:2, 0:1]
    norm_w = par_ref[2:3, :]

    def conv(scr, w_ref, b_ref, r0):
        acc = b_ref[...] + scr[pl.ds(r0 + 5, c), :] * w_ref[0:1, :]
        for j in range(1, CONV_W):
            acc = acc + scr[pl.ds(r0 + 5 + j, c), :] * w_ref[j:j + 1, :]
        return acc

    def prep(ci, carry):
        r0 = pl.multiple_of(ci * c, c)
        qc = _silu(conv(scq, wq_ref, bq_ref, r0))
        kc = _silu(conv(sck, wk_ref, bk_ref, r0))
        vc = _silu(conv(scv, wv_ref, bv_ref, r0))
        qn = qc * lax.rsqrt(jnp.sum(qc * qc, -1, keepdims=True) + NORM_EPS) * SCALE
        kn = kc * lax.rsqrt(jnp.sum(kc * kc, -1, keepdims=True) + NORM_EPS)
        sm = sm_ref[pl.ds(r0, c), :]
        a_col = jnp.sum(jnp.where(lane == h, sm, 0.0), -1, keepdims=True)
        b_col = jnp.sum(jnp.where(lane == N_HEADS + h, sm, 0.0), -1, keepdims=True)
        beta = jax.nn.sigmoid(b_col)
        g = neg_a * jax.nn.softplus(a_col + dt_bias)
        gcb = _mm(trilf, jnp.broadcast_to(g, (c, LANE)), HI)
        gc = gcb[:, 0:1]
        gcr = _nt(e0, gcb, HI)
        decay = jnp.where(tril, jnp.exp(jnp.minimum(gcb[:, :c] - gcr, 0.0)), 0.0)
        q16, k16 = _b(qn), _b(kn)
        kk = _nt(k16, k16)
        a = jnp.where(strict, beta * kk * decay, 0.0)
        x = -a
        p = eye + x
        for _ in range(int(math.log2(c)) - 1):
            x = _mm(x, x, HI)
            p = _mm(p, eye + x, HI)
        egc = jnp.exp(gc)
        u = _mm(p, vc * beta, HI)
        w = _mm(p, kn * (beta * egc), HI)
        gl = gcb[c - 1:c, :]
        u_s[ci] = u
        w_s[ci] = _b(w)
        qk_s[ci] = _nt(q16, k16) * decay
        qd_s[ci] = _b(qn * egc)
        kt_s[ci] = kn * jnp.exp(gl[:, 0:1] - gc)
        gl_s[ci] = jnp.exp(gl)
        return carry

    lax.fori_loop(0, nc, prep, 0)

    def step(ci, s):
        r0 = pl.multiple_of(ci * c, c)
        s16 = _b(s)
        v_new = u_s[ci] - _mm(w_s[ci], s16)
        o = _mm(qd_s[ci], s16) + _mm(_b(qk_s[ci]), _b(v_new))
        s = s * gl_s[ci] + _tn(kt_s[ci], v_new)
        o = o * lax.rsqrt(jnp.mean(o * o, -1, keepdims=True) + NORM_EPS) * norm_w
        y_ref[pl.ds(r0, c), :] = o * _silu(z_ref[pl.ds(r0, c), :])
        return s

    sfin_ref[...] = lax.fori_loop(0, nc, step, s0_ref[...])


def _gdn(h3, hist8, conv_w, conv_b, par, s0):
    bsz, t, _ = h3.shape
    c = math.gcd(t, CHUNK)
    nc = t // c
    col = lambda blk: pl.BlockSpec((None, t, LANE), lambda b, h, blk=blk: (b, 0, blk + h))
    wsp = lambda blk: pl.BlockSpec((CONV_W, LANE), lambda b, h, blk=blk: (0, blk + h))
    bsp = lambda blk: pl.BlockSpec((1, LANE), lambda b, h, blk=blk: (0, blk + h))
    hsp = lambda blk: pl.BlockSpec((None, 8, LANE), lambda b, h, blk=blk: (b, 0, blk + h))
    st = pl.BlockSpec((None, None, HEAD_DIM, HEAD_DIM), lambda b, h: (b, h, 0, 0))
    return pl.pallas_call(
        functools.partial(_gdn_kernel, t=t, c=c),
        grid=(bsz, N_HEADS),
        in_specs=[col(0), col(4), col(8), col(C_GDNZ // LANE),
                  pl.BlockSpec((None, t, LANE), lambda b, h: (b, 0, C_SMALL // LANE)),
                  wsp(0), wsp(4), wsp(8), bsp(0), bsp(4), bsp(8), hsp(0), hsp(4), hsp(8),
                  pl.BlockSpec((None, 8, LANE), lambda b, h: (h, 0, 0)), st],
        out_specs=[pl.BlockSpec((None, t, LANE), lambda b, h: (b, 0, h)), st],
        out_shape=[jax.ShapeDtypeStruct((bsz, t, BRANCH_W), F32),
                   jax.ShapeDtypeStruct((bsz, N_HEADS, HEAD_DIM, HEAD_DIM), F32)],
        scratch_shapes=[pltpu.VMEM((t + 8, LANE), F32)] * 3 + [
            pltpu.VMEM((nc, c, LANE), F32), pltpu.VMEM((nc, c, LANE), BF16), pltpu.VMEM((nc, c, c), F32),
            pltpu.VMEM((nc, c, LANE), BF16), pltpu.VMEM((nc, c, LANE), F32), pltpu.VMEM((nc, 1, LANE), F32)],
        compiler_params=_cp(2),
        name="gdn",
    )(h3, h3, h3, h3, h3, conv_w, conv_w, conv_w, conv_b, conv_b, conv_b, hist8, hist8, hist8, par, s0)


def _ret_kernel(q_ref, k_ref, v_ref, z_ref, cos_ref, sin_ref, par_ref, r0_ref, y_ref, rfin_ref, *, t, c):
    nc = t // c
    lg = jnp.log(par_ref[0:1, 0:1])
    ii = lax.broadcasted_iota(jnp.int32, (c, c), 0)
    jj = lax.broadcasted_iota(jnp.int32, (c, c), 1)
    dmat = jnp.where(ii >= jj, jnp.exp((ii - jj).astype(F32) * lg), 0.0)
    icol = lax.broadcasted_iota(jnp.int32, (c, 1), 0).astype(F32)
    xi = jnp.exp((icol + 1.0) * lg)
    zeta = jnp.exp((c - 1.0 - icol) * lg)
    g_c = jnp.exp(c * lg)

    def step(ci, r):
        rows = pl.ds(pl.multiple_of(ci * c, c), c)
        cos, sin = cos_ref[rows, :], sin_ref[rows, :]
        q = _rope128(q_ref[rows, :], cos, sin)
        k = _rope128(k_ref[rows, :], cos, sin) * SCALE
        v = v_ref[rows, :]
        q16, k16, v16 = _b(q), _b(k), _b(v)
        o = _mm(_b(_nt(q16, k16) * dmat), v16) + _mm(_b(q * xi), _b(r))
        r = r * g_c + _tn(k * zeta, v)
        mu = jnp.mean(o, -1, keepdims=True)
        d = o - mu
        var = jnp.mean(d * d, -1, keepdims=True)
        y_ref[rows, :] = d * lax.rsqrt(var + LN_EPS) * _silu(z_ref[rows, :])
        return r

    rfin_ref[...] = lax.fori_loop(0, nc, step, r0_ref[...])


def _ret_par():
    gamma = (1.0 - 2.0 ** (-5.0 - np.arange(N_HEADS, dtype=np.float64))).astype(np.float32)
    return jnp.asarray(np.broadcast_to(gamma[:, None, None], (N_HEADS, 8, LANE)).copy(), F32)


def _ret(h3, cos, sin, par, r0):
    bsz, t, _ = h3.shape
    c = math.gcd(t, CHUNK)
    col = lambda blk: pl.BlockSpec((None, t, LANE), lambda b, h, blk=blk: (b, 0, blk + h))
    tab = pl.BlockSpec((t, LANE), lambda b, h: (0, 0))
    st = pl.BlockSpec((None, None, HEAD_DIM, HEAD_DIM), lambda b, h: (b, h, 0, 0))
    base = C_RET // LANE
    return pl.pallas_call(
        functools.partial(_ret_kernel, t=t, c=c),
        grid=(bsz, N_HEADS),
        in_specs=[col(base), col(base + 4), col(base + 8), col(base + 12), tab, tab,
                  pl.BlockSpec((None, 8, LANE), lambda b, h: (h, 0, 0)), st],
        out_specs=[pl.BlockSpec((None, t, LANE), lambda b, h: (b, 0, h)), st],
        out_shape=[jax.ShapeDtypeStruct((bsz, t, BRANCH_W), F32),
                   jax.ShapeDtypeStruct((bsz, N_HEADS, HEAD_DIM, HEAD_DIM), F32)],
        compiler_params=_cp(2),
        name="retention",
    )(h3, h3, h3, h3, cos, sin, par, r0)


def _lru_gates(x, wr_ref, wi_ref, br, bi, lam):
    rs, is_ = [], []
    for h in range(N_HEADS):
        sl = slice(h * HEAD_DIM, (h + 1) * HEAD_DIM)
        xh = _b(x[:, sl])
        rs.append(_mm(xh, wr_ref[h]))
        is_.append(_mm(xh, wi_ref[h]))
    r = jax.nn.sigmoid(jnp.concatenate(rs, 1) + br)
    ig = jax.nn.sigmoid(jnp.concatenate(is_, 1) + bi)
    log_a = -LRU_C * r * jax.nn.softplus(-lam)
    a = jnp.exp(log_a)
    th = jnp.tanh(log_a)
    one_m_a2 = -2.0 * th / (1.0 - th)
    return a, jnp.sqrt(one_m_a2) * (ig * x)


def _lru_kernel(x_ref, z_ref, w_ref, b_ref, hist_ref, wr_ref, wi_ref, br_ref, bi_ref, lam_ref, h0_ref,
                y_ref, hfin_ref, scx, a_s, b_s, *, t, rb):
    for h in range(N_HEADS):
        sl = slice(h * HEAD_DIM, (h + 1) * HEAD_DIM)
        scx[h, pl.ds(0, 8), :] = hist_ref[:, sl]
        scx[h, pl.ds(8, t), :] = x_ref[:, sl]
    br, bi, lam = br_ref[...], bi_ref[...], lam_ref[...]

    def gates(i, carry):
        r0 = pl.multiple_of(i * rb, rb)
        xs = []
        for h in range(N_HEADS):
            sl = slice(h * HEAD_DIM, (h + 1) * HEAD_DIM)
            xh = b_ref[:, sl] + scx[h, pl.ds(r0 + 5, rb), :] * w_ref[0:1, sl]
            for j in range(1, CONV_W):
                xh = xh + scx[h, pl.ds(r0 + 5 + j, rb), :] * w_ref[j:j + 1, sl]
            xs.append(xh)
        x = jnp.concatenate(xs, 1)
        a, b = _lru_gates(x, wr_ref, wi_ref, br, bi, lam)
        a_s[pl.ds(r0, rb), :] = a
        b_s[pl.ds(r0, rb), :] = b
        return carry

    lax.fori_loop(0, t // rb, gates, 0)

    def scan(i, h):
        h = a_s[pl.ds(i, 1), :] * h + b_s[pl.ds(i, 1), :]
        b_s[pl.ds(i, 1), :] = h
        return h

    hfin_ref[...] = lax.fori_loop(0, t, scan, h0_ref[...], unroll=8)

    def gate_out(i, carry):
        rows = pl.ds(pl.multiple_of(i * rb, rb), rb)
        y_ref[rows, :] = b_s[rows, :] * _silu(z_ref[rows, :])
        return carry

    lax.fori_loop(0, t // rb, gate_out, 0)


def _lru(h3, hist8, conv_w, conv_b, w_r, w_i, b_r, b_i, lam, h0):
    bsz, t, _ = h3.shape
    rb = math.gcd(t, 256)
    xblk = (C_CONV + 3 * BRANCH_W) // BRANCH_W
    vec = pl.BlockSpec((1, BRANCH_W), lambda b: (0, 0))
    wsp = pl.BlockSpec((N_HEADS, HEAD_DIM, HEAD_DIM), lambda b: (0, 0, 0))
    return pl.pallas_call(
        functools.partial(_lru_kernel, t=t, rb=rb),
        grid=(bsz,),
        in_specs=[pl.BlockSpec((None, t, BRANCH_W), lambda b: (b, 0, xblk)),
                  pl.BlockSpec((None, t, BRANCH_W), lambda b: (b, 0, C_LRUZ // BRANCH_W)),
                  pl.BlockSpec((CONV_W, BRANCH_W), lambda b: (0, xblk)),
                  pl.BlockSpec((1, BRANCH_W), lambda b: (0, xblk)),
                  pl.BlockSpec((None, 8, BRANCH_W), lambda b: (b, 0, xblk)),
                  wsp, wsp, vec, vec, vec,
                  pl.BlockSpec((None, 1, BRANCH_W), lambda b: (b, 0, 0))],
        out_specs=[pl.BlockSpec((None, t, BRANCH_W), lambda b: (b, 0, 0)),
                   pl.BlockSpec((None, 1, BRANCH_W), lambda b: (b, 0, 0))],
        out_shape=[jax.ShapeDtypeStruct((bsz, t, BRANCH_W), F32), jax.ShapeDtypeStruct((bsz, 1, BRANCH_W), F32)],
        scratch_shapes=[pltpu.VMEM((N_HEADS, t + 8, HEAD_DIM), F32), pltpu.VMEM((t, BRANCH_W), F32),
                        pltpu.VMEM((t, BRANCH_W), F32)],
        compiler_params=_cp(1),
        name="rglru",
    )(h3, h3, conv_w, conv_b, hist8, w_r, w_i, b_r, b_i, lam, h0)


BISECT_STEPS = 18


def _kth_threshold(count_ge, count_gt, next_above, lo, hi, k):
    c_hi = count_ge(hi)
    top = c_hi >= k
    lo = jnp.where(top, hi, lo)

    def halve(_, c):
        lo, hi = c
        mid = lo + (hi - lo) * 0.5
        up = count_ge(mid) >= k
        return jnp.where(up, mid, lo), jnp.where(up, hi, mid)

    lo, _ = lax.fori_loop(0, BISECT_STEPS, halve, (lo, hi))

    def more(c):
        return jnp.max(c[1] - k) > 0.5

    def walk(c):
        thr, c_gt = c
        nxt = jnp.where(c_gt > k, next_above(thr), thr)
        return nxt, count_gt(nxt)

    return lax.while_loop(more, walk, (lo, count_gt(lo)))


def _dsa_kernel(q_ref, z_ref, qi_ref, sm_ref, cos_ref, sin_ref, ci_ref, sia_ref, sib_ref,
                k_ref, v_ref, ki_ref, tri_ref, o_ref, sc_s, bias_s, *, tq, t, ksel):
    i = pl.program_id(1)
    ki16 = ki_ref[...]
    sm = sm_ref[...]
    ci, sia, sib = ci_ref[...], sia_ref[...], sib_ref[...]
    score = jnp.zeros((tq, t), F32)
    for h in range(IDX_HEADS):
        x = _rope_idx(qi_ref[:, h * LANE:(h + 1) * LANE], ci, sia, sib)
        s = _nt(_b(x), ki16)
        w = sm[:, 8 + h:9 + h] * IDX_SCALE
        score = score + jnp.maximum(s, 0.0) * w
    qpos = i * tq + lax.broadcasted_iota(jnp.int32, (tq, 1), 0)
    kpos = lax.broadcasted_iota(jnp.int32, (1, t), 1)
    valid = kpos <= qpos
    sc_s[...] = jnp.where(valid, score, -jnp.inf)
    k_row = jnp.minimum(float(ksel), (qpos + 1).astype(F32))

    def count_ge(th):
        return jnp.sum(jnp.where(sc_s[...] >= th, 1.0, 0.0), -1, keepdims=True)

    def count_gt(th):
        return jnp.sum(jnp.where(sc_s[...] > th, 1.0, 0.0), -1, keepdims=True)

    def next_above(th):
        s = sc_s[...]
        return jnp.min(jnp.where(s > th, s, jnp.inf), -1, keepdims=True)

    lo = jnp.min(jnp.where(valid, score, jnp.inf), -1, keepdims=True)
    hi = jnp.max(sc_s[...], -1, keepdims=True)
    thr, n_gt = _kth_threshold(count_ge, count_gt, next_above, lo, hi, k_row)
    need = k_row - n_gt
    run = jnp.zeros((tq, 1), F32)
    tri = tri_ref[...]
    for cb in range(t // LANE):
        sl = slice(cb * LANE, (cb + 1) * LANE)
        sc = sc_s[:, sl]
        eq = sc == thr
        pre = _mm(jnp.where(eq, 1.0, 0.0).astype(BF16), tri) + run
        bias_s[:, sl] = jnp.where(sc > thr, 0.0, jnp.where(eq, jnp.where(pre <= need, 0.0, NEG), NEG))
        run = pre[:, LANE - 1:LANE]
    bias = bias_s[...]
    cos, sin = cos_ref[...], sin_ref[...]
    for h in range(N_HEADS):
        sl = slice(h * HEAD_DIM, (h + 1) * HEAD_DIM)
        q = _rope128(q_ref[:, sl], cos, sin)
        lg = _nt(_b(q), k_ref[:, sl]) * SCALE + bias
        m = jnp.max(lg, -1, keepdims=True)
        p = jnp.exp(lg - m)
        l = jnp.sum(p, -1, keepdims=True)
        o = _mm(_b(p), v_ref[:, sl]) / l
        o_ref[:, sl] = o * _silu(z_ref[:, sl])


def _tri128():
    return jnp.asarray(np.triu(np.ones((LANE, LANE), np.float32)), BF16)


def _dsa(h3, k16, v16, ki16, tabs, tq):
    bsz, t, _ = h3.shape
    ksel = min(DSA_TOPK, t // 4)
    cos, sin, ci, sia, sib = tabs
    qtab = pl.BlockSpec((tq, LANE), lambda b, i: (i, 0))
    full = lambda w: pl.BlockSpec((None, t, w), lambda b, i: (b, 0, 0))
    return pl.pallas_call(
        functools.partial(_dsa_kernel, tq=tq, t=t, ksel=ksel),
        grid=(bsz, t // tq),
        in_specs=[pl.BlockSpec((None, tq, 512), lambda b, i: (b, i, C_DSA // 512)),
                  pl.BlockSpec((None, tq, 512), lambda b, i: (b, i, (C_DSA + 1536) // 512)),
                  pl.BlockSpec((None, tq, 1024), lambda b, i: (b, i, C_QI // 1024)),
                  pl.BlockSpec((None, tq, LANE), lambda b, i: (b, i, C_SMALL // LANE)),
                  qtab, qtab, qtab, qtab, qtab,
                  full(512), full(512), full(LANE),
                  pl.BlockSpec((LANE, LANE), lambda b, i: (0, 0))],
        out_specs=pl.BlockSpec((None, tq, 512), lambda b, i: (b, i, 0)),
        out_shape=jax.ShapeDtypeStruct((bsz, t, BRANCH_W), F32),
        scratch_shapes=[pltpu.VMEM((tq, t), F32), pltpu.VMEM((tq, t), F32)],
        compiler_params=_cp(2),
        name="dsa_prompt",
    )(h3, h3, h3, h3, cos, sin, ci, sia, sib, k16, v16, ki16, _tri128())


def _mem_kernel(q_ref, z_ref, mk_ref, mv_ref, o_ref):
    for h in range(N_HEADS):
        sl = slice(h * HEAD_DIM, (h + 1) * HEAD_DIM)
        lg = _nt(_b(q_ref[:, sl]), _b(mk_ref[:, sl])) * SCALE
        m = jnp.max(lg, -1, keepdims=True)
        p = jnp.exp(lg - m)
        l = jnp.sum(p, -1, keepdims=True)
        o = _mm(_b(p), _b(mv_ref[:, sl])) / l
        o_ref[:, sl] = o * _silu(z_ref[:, sl])


def _mem(h3, mkv3, tq):
    bsz, t, _ = h3.shape
    nm = mkv3.shape[1]
    return pl.pallas_call(
        _mem_kernel,
        grid=(bsz, t // tq),
        in_specs=[pl.BlockSpec((None, tq, 512), lambda b, i: (b, i, C_MEM // 512)),
                  pl.BlockSpec((None, tq, 512), lambda b, i: (b, i, C_MEM // 512 + 1)),
                  pl.BlockSpec((None, nm, 512), lambda b, i: (b, 0, 0)),
                  pl.BlockSpec((None, nm, 512), lambda b, i: (b, 0, 1))],
        out_specs=pl.BlockSpec((None, tq, 512), lambda b, i: (b, i, 0)),
        out_shape=jax.ShapeDtypeStruct((bsz, t, BRANCH_W), F32),
        compiler_params=_cp(2),
        name="mem_attn",
    )(h3, h3, mkv3, mkv3)


def _merge_kernel(ya_ref, yb_ref, yc_ref, yd_ref, ye_ref, g_ref, w_ref, o_ref):
    acc = None
    for nbr, y_ref in enumerate((ya_ref, yb_ref, yc_ref, yd_ref, ye_ref)):
        p = _mm(_b(y_ref[...]), w_ref[nbr])
        g = jax.nn.sigmoid(g_ref[:, nbr * D_MODEL:(nbr + 1) * D_MODEL].astype(F32))
        acc = g * p if acc is None else acc + g * p
    o_ref[...] = _b(acc)


def _merge(ys, g, w_br, tm):
    m = g.shape[0]
    ysp = pl.BlockSpec((tm, BRANCH_W), lambda i: (i, 0))
    return pl.pallas_call(
        _merge_kernel,
        grid=(m // tm,),
        in_specs=[ysp] * N_BRANCH + [pl.BlockSpec((tm, N_GATE), lambda i: (i, 0)),
                                     pl.BlockSpec((N_BRANCH, BRANCH_W, D_MODEL), lambda i: (0, 0, 0))],
        out_specs=pl.BlockSpec((tm, D_MODEL), lambda i: (i, 0)),
        out_shape=jax.ShapeDtypeStruct((m, D_MODEL), BF16),
        compiler_params=_cp(1),
        name="gated_merge",
    )(*ys, g, w_br)


def _outnorm_kernel(m_ref, x_ref, w_ref, g_ref, b_ref, y_ref, y16_ref):
    v = DEEPNORM_ALPHA * x_ref[...] + _mm(m_ref[...], w_ref[...])
    mu = jnp.mean(v, -1, keepdims=True)
    d = v - mu
    var = jnp.mean(d * d, -1, keepdims=True)
    y = d * lax.rsqrt(var + LN_EPS) * g_ref[...] + b_ref[...]
    y_ref[...] = y
    y16_ref[...] = _b(y)


def _outnorm(merged, x, w_out, ln_g, ln_b, tm):
    m = x.shape[0]
    row = pl.BlockSpec((tm, D_MODEL), lambda i: (i, 0))
    vec = pl.BlockSpec((1, D_MODEL), lambda i: (0, 0))
    return pl.pallas_call(
        _outnorm_kernel,
        grid=(m // tm,),
        in_specs=[row, row, pl.BlockSpec((D_MODEL, D_MODEL), lambda i: (0, 0)), vec, vec],
        out_specs=[row, row],
        out_shape=[jax.ShapeDtypeStruct((m, D_MODEL), F32), jax.ShapeDtypeStruct((m, D_MODEL), BF16)],
        compiler_params=_cp(1),
        name="out_norm",
    )(merged, x, w_out, ln_g, ln_b)


def _layer_params(l, w_in, conv_w, conv_b, gdn_a_log, gdn_dt_bias, gdn_norm_w, lru_lambda, lru_w_r, lru_b_r,
                  lru_w_i, lru_b_i, w_mem_kv, w_branch, w_out, ln_g, ln_b):
    w_main, w_gate = _prep_w_in(w_in[l])
    rows = jnp.stack([jnp.broadcast_to(gdn_a_log[l][:, None], (N_HEADS, LANE)),
                      jnp.broadcast_to(gdn_dt_bias[l][:, None], (N_HEADS, LANE)),
                      jnp.broadcast_to(gdn_norm_w[l][None, :], (N_HEADS, LANE))], 1)
    gdn_par = jnp.concatenate([rows, jnp.zeros((N_HEADS, 5, LANE), F32)], 1)
    return dict(
        w_main=w_main, w_gate=w_gate, conv_w=conv_w[l], conv_b=conv_b[l][None, :], gdn_par=gdn_par,
        ret_par=_ret_par(), lru_w_r=_b(lru_w_r[l]), lru_w_i=_b(lru_w_i[l]),
        lru_b_r=lru_b_r[l][None, :], lru_b_i=lru_b_i[l][None, :], lam=lru_lambda[l][None, :],
        w_mem_kv=_b(w_mem_kv[l]), w_branch=_b(w_branch[l]), w_out=_b(w_out[l]),
        ln_g=ln_g[l][None, :], ln_b=ln_b[l][None, :])


def _hist8(buf):
    return jnp.concatenate([jnp.zeros((buf.shape[0], 8 - (CONV_W - 1), buf.shape[2]), buf.dtype), buf], 1)


def _prompt_layer(x, x16, mem16, p, tabs):
    bsz, t, _ = x.shape
    m = bsz * t
    tm = math.gcd(m, 1024)
    h = _matmul(x16, p["w_main"], F32, tm, 512)
    g = _matmul(x16, p["w_gate"], BF16, tm, 1024)
    mkv = _matmul(mem16, p["w_mem_kv"], F32, math.gcd(mem16.shape[0], 512), 512)
    h3 = h.reshape(bsz, t, N_MAIN)
    mkv3 = mkv.reshape(bsz, -1, 2 * BRANCH_W)
    zero_hist = jnp.zeros((bsz, 8, CONV_DIM), F32)
    zero_state = jnp.zeros((bsz, N_HEADS, HEAD_DIM, HEAD_DIM), F32)
    cos, sin = tabs[0], tabs[1]
    y_a, s_gdn = _gdn(h3, zero_hist, p["conv_w"], p["conv_b"], p["gdn_par"], zero_state)
    kb, k16, v16, kif, ki16 = _rope_k(h3, tabs, math.gcd(t, 512))
    y_b = _dsa(h3, k16, v16, ki16, tabs, math.gcd(t, 256))
    y_c, s_ret = _ret(h3, cos, sin, p["ret_par"], zero_state)
    y_d, s_lru = _lru(h3, zero_hist, p["conv_w"], p["conv_b"], p["lru_w_r"], p["lru_w_i"], p["lru_b_r"],
                      p["lru_b_i"], p["lam"], jnp.zeros((bsz, 1, BRANCH_W), F32))
    y_e = _mem(h3, mkv3, math.gcd(t, 512))
    ys = [a.reshape(m, BRANCH_W) for a in (y_a, y_b, y_c, y_d, y_e)]
    merged = _merge(ys, g, p["w_branch"], math.gcd(m, 256))
    y, y16 = _outnorm(merged, x.reshape(m, D_MODEL), p["w_out"], p["ln_g"], p["ln_b"], math.gcd(m, 512))
    new_conv = h3[:, t - (CONV_W - 1):, C_CONV:C_CONV + CONV_DIM]
    v_rows = h3[:, :, C_DSA + 1024:C_DSA + 1536].reshape(bsz, t, N_HEADS, HEAD_DIM)
    mk = mkv3[:, :, :BRANCH_W].reshape(bsz, -1, N_HEADS, HEAD_DIM)
    mv = mkv3[:, :, BRANCH_W:].reshape(bsz, -1, N_HEADS, HEAD_DIM)
    return (y.reshape(bsz, t, D_MODEL), y16,
            (new_conv, s_gdn, s_ret, s_lru.reshape(bsz, BRANCH_W)),
            (kb.reshape(bsz, t, N_HEADS, HEAD_DIM), v_rows, kif[:, :, :IDX_DIM]), mk, mv)


def _dec_kernel(hs_ref, hist_ref, cw_ref, cb_ref, gpar_ref, rpar_ref, sg_ref, sr_ref, hl_ref,
                wr_ref, wi_ref, br_ref, bi_ref, lam_ref, mk_ref, mv_ref,
                cos_ref, sin_ref, ci_ref, sia_ref, sib_ref,
                ya_ref, yc_ref, yd_ref, ye_ref, sgo_ref, sro_ref, hlo_ref, qb_ref, kb_ref, qi_ref, ki_ref, ss_ref,
                wk_s, vb_s, qd_s, qkg_s, eg_s, kg_s, qx_s, qkr_s, kr_s, vr_s, mq_s, oa_s, oc_s, oe_s, *, nb):
    hd = HEAD_DIM
    hsl = lambda h: slice(h * hd, (h + 1) * hd)
    x_in = hs_ref[:, C_CONV:C_CONV + CONV_DIM]
    conv = cb_ref[...] + x_in * cw_ref[CONV_W - 1:CONV_W, :]
    for j in range(CONV_W - 1):
        conv = conv + hist_ref[j] * cw_ref[j:j + 1, :]
    sm = hs_ref[:, C_SMALL:C_SMALL + LANE]
    cos, sin = cos_ref[...], sin_ref[...]
    ci, sia, sib = ci_ref[...], sia_ref[...], sib_ref[...]
    ones = jnp.ones((1, hd), F32)
    for h in range(N_HEADS):
        sl = hsl(h)
        q = _silu(conv[:, h * hd:(h + 1) * hd])
        k = _silu(conv[:, BRANCH_W + h * hd:BRANCH_W + (h + 1) * hd])
        v = _silu(conv[:, 2 * BRANCH_W + h * hd:2 * BRANCH_W + (h + 1) * hd])
        q = q * lax.rsqrt(jnp.sum(q * q, -1, keepdims=True) + NORM_EPS) * SCALE
        k = k * lax.rsqrt(jnp.sum(k * k, -1, keepdims=True) + NORM_EPS)
        beta = jax.nn.sigmoid(sm[:, N_HEADS + h:N_HEADS + h + 1])
        g = -jnp.exp(gpar_ref[h, 0:1, 0:1]) * jax.nn.softplus(sm[:, h:h + 1] + gpar_ref[h, 1:2, 0:1])
        eg = jnp.exp(g)
        wk_s[h] = k * (beta * eg)
        vb_s[h] = v * beta
        qd_s[h] = q * eg
        qkg_s[h] = jnp.sum(q * k, -1, keepdims=True) * ones
        eg_s[h] = eg * ones
        kg_s[h] = k
        q = _rope128(hs_ref[:, C_RET + h * hd:C_RET + (h + 1) * hd], cos, sin)
        k = _rope128(hs_ref[:, C_RET + BRANCH_W + h * hd:C_RET + BRANCH_W + (h + 1) * hd], cos, sin) * SCALE
        qx_s[h] = q * rpar_ref[h, 0:1, 0:1]
        qkr_s[h] = jnp.sum(q * k, -1, keepdims=True) * ones
        kr_s[h] = k
        vr_s[h] = hs_ref[:, C_RET + 2 * BRANCH_W + h * hd:C_RET + 2 * BRANCH_W + (h + 1) * hd]
        mq_s[h] = hs_ref[:, C_MEM + h * hd:C_MEM + (h + 1) * hd]
        qb_ref[:, sl] = _rope128(hs_ref[:, C_DSA + h * hd:C_DSA + (h + 1) * hd], cos, sin)
        kb_ref[:, sl] = _rope128(hs_ref[:, C_DSA + BRANCH_W + h * hd:C_DSA + BRANCH_W + (h + 1) * hd], cos, sin)
    ki = _rope_idx(hs_ref[:, C_KI:C_KI + LANE], ci, sia, sib)
    ki_ref[...] = ki
    sself = jnp.zeros((nb, 1), F32)
    for h in range(IDX_HEADS):
        qi = _rope_idx(hs_ref[:, C_QI + h * LANE:C_QI + (h + 1) * LANE], ci, sia, sib)
        qi_ref[:, h * LANE:(h + 1) * LANE] = qi
        s = jnp.sum(qi * ki, -1, keepdims=True)
        sself = sself + jnp.maximum(s, 0.0) * (sm[:, 8 + h:9 + h] * IDX_SCALE)
    ss_ref[...] = sself * jnp.ones((1, LANE), F32)

    ii = lax.broadcasted_iota(jnp.int32, (hd, hd), 0)
    jj = lax.broadcasted_iota(jnp.int32, (hd, hd), 1)
    eye = jnp.where(ii == jj, 1.0, 0.0).astype(F32)

    def col_bcast(row):
        return _nt(eye, jnp.broadcast_to(row, (hd, hd)), HI)

    def per_batch(bb, carry):
        r = pl.ds(bb, 1)
        for h in range(N_HEADS):
            sl = hsl(h)
            s = sg_ref[bb, h]
            v_new = vb_s[h, r, :] - _mm(wk_s[h, r, :], s, HI)
            oa_s[h, r, :] = _mm(qd_s[h, r, :], s, HI) + qkg_s[h, r, :] * v_new
            sgo_ref[bb, h] = s * eg_s[h, r, :] + col_bcast(kg_s[h, r, :]) * v_new
            rs = sr_ref[bb, h]
            v = vr_s[h, r, :]
            oc_s[h, r, :] = _mm(qx_s[h, r, :], rs, HI) + qkr_s[h, r, :] * v
            sro_ref[bb, h] = rs * rpar_ref[h, 0:1, :] + col_bcast(kr_s[h, r, :]) * v
            lg = _nt(_b(mq_s[h, r, :]), _b(mk_ref[bb, :, sl])) * SCALE
            m = jnp.max(lg, -1, keepdims=True)
            p = jnp.exp(lg - m)
            oe_s[h, r, :] = _mm(_b(p), _b(mv_ref[bb, :, sl])) / jnp.sum(p, -1, keepdims=True)
        return carry

    lax.fori_loop(0, nb, per_batch, 0)

    for h in range(N_HEADS):
        sl = hsl(h)
        o = oa_s[h]
        o = o * lax.rsqrt(jnp.mean(o * o, -1, keepdims=True) + NORM_EPS) * gpar_ref[h, 2:3, :]
        ya_ref[:, sl] = o * _silu(hs_ref[:, C_GDNZ + h * hd:C_GDNZ + (h + 1) * hd])
        o = oc_s[h]
        d = o - jnp.mean(o, -1, keepdims=True)
        var = jnp.mean(d * d, -1, keepdims=True)
        yc_ref[:, sl] = d * lax.rsqrt(var + LN_EPS) * _silu(hs_ref[:, C_RET + 3 * BRANCH_W + h * hd:C_RET + 3 * BRANCH_W + (h + 1) * hd])
        ye_ref[:, sl] = oe_s[h] * _silu(hs_ref[:, C_MEM + BRANCH_W + h * hd:C_MEM + BRANCH_W + (h + 1) * hd])
    a, b = _lru_gates(conv[:, 3 * BRANCH_W:], wr_ref, wi_ref, br_ref[...], bi_ref[...], lam_ref[...])
    h_new = a * hl_ref[...] + b
    hlo_ref[...] = h_new
    yd_ref[...] = h_new * _silu(hs_ref[:, C_LRUZ:C_LRUZ + BRANCH_W])


def _decode(hs, hist_t, p, sg, sr, hl, mk3, mv3, tabs):
    nb = hs.shape[0]
    bw = jax.ShapeDtypeStruct((nb, BRANCH_W), F32)
    st = jax.ShapeDtypeStruct((nb, N_HEADS, HEAD_DIM, HEAD_DIM), F32)
    out_shape = [bw, bw, bw, bw, st, st, bw, bw, bw,
                 jax.ShapeDtypeStruct((nb, IDX_HEADS * LANE), F32), jax.ShapeDtypeStruct((nb, LANE), F32),
                 jax.ShapeDtypeStruct((nb, LANE), F32)]
    return pl.pallas_call(
        functools.partial(_dec_kernel, nb=nb),
        out_shape=out_shape,
        scratch_shapes=[pltpu.VMEM((N_HEADS, nb, HEAD_DIM), F32)] * 14,
        compiler_params=pltpu.CompilerParams(vmem_limit_bytes=VMEM_LIMIT),
        name="decode_step",
    )(hs, hist_t, p["conv_w"], p["conv_b"], p["gdn_par"], p["ret_par"], sg, sr, hl,
      p["lru_w_r"], p["lru_w_i"], p["lru_b_r"], p["lru_b_i"], p["lam"], mk3, mv3, *tabs)


PAGES_PER_STEP = 8


def _didx_kernel(pt_ref, qi_ref, w_ref, *rest):
    pages, o_ref = rest[:PAGES_PER_STEP], rest[PAGES_PER_STEP]
    q16 = _b(qi_ref[:, :IDX_DIM])
    w = w_ref[...]
    for j, pg in enumerate(pages):
        s = _nt(q16, _b(pg[...]))
        o_ref[j:j + 1, :] = jnp.sum(jnp.maximum(s, 0.0) * w, 0, keepdims=True)


def _didx(page_table, qi3, wcol, pool_ik):
    nb, n_pages = page_table.shape
    ng = n_pages // PAGES_PER_STEP
    page = lambda j: pl.BlockSpec((None, PAGE_SIZE, IDX_DIM), lambda b, g, pt, j=j: (pt[b, g * PAGES_PER_STEP + j], 0, 0))
    return pl.pallas_call(
        _didx_kernel,
        grid_spec=pltpu.PrefetchScalarGridSpec(
            num_scalar_prefetch=1, grid=(nb, ng),
            in_specs=[pl.BlockSpec((None, IDX_HEADS, LANE), lambda b, g, pt: (b, 0, 0)),
                      pl.BlockSpec((None, IDX_HEADS, 1), lambda b, g, pt: (b, 0, 0))]
            + [page(j) for j in range(PAGES_PER_STEP)],
            out_specs=pl.BlockSpec((None, None, PAGES_PER_STEP, PAGE_SIZE), lambda b, g, pt: (b, g, 0, 0))),
        out_shape=jax.ShapeDtypeStruct((nb, ng, PAGES_PER_STEP, PAGE_SIZE), F32),
        compiler_params=_cp(2),
        name="paged_indexer",
    )(page_table, qi3, wcol, *([pool_ik] * PAGES_PER_STEP))


NEG_SEL = -2e30
M_INIT = -1e30


def _dsel_kernel(sc_ref, ss_ref, tri_ref, bias_ref, sbias_ref, *, n_pages, ksel):
    key = sc_ref[...]
    kself = ss_ref[:, 0:1]

    def reduce2(x, op):
        return op(op(x, 1, keepdims=True), 0, keepdims=True)

    def count_ge(th):
        return reduce2(jnp.where(key >= th, 1.0, 0.0), jnp.sum) + jnp.where(kself >= th, 1.0, 0.0)

    def count_gt(th):
        return reduce2(jnp.where(key > th, 1.0, 0.0), jnp.sum) + jnp.where(kself > th, 1.0, 0.0)

    def next_above(th):
        return jnp.minimum(reduce2(jnp.where(key > th, key, jnp.inf), jnp.min), jnp.where(kself > th, kself, jnp.inf))

    lo = jnp.minimum(reduce2(key, jnp.min), kself)
    hi = jnp.maximum(reduce2(key, jnp.max), kself)
    thr, n_gt = _kth_threshold(count_ge, count_gt, next_above, lo, hi, float(ksel))
    eq = key == thr
    need = ksel - n_gt
    pre_in = _mm(jnp.where(eq, 1.0, 0.0).astype(BF16), tri_ref[...])
    tot = pre_in[:, PAGE_SIZE - 1:PAGE_SIZE]
    ii = lax.broadcasted_iota(jnp.int32, (n_pages, n_pages), 0)
    jj = lax.broadcasted_iota(jnp.int32, (n_pages, n_pages), 1)
    below = jnp.where(ii > jj, 1.0, 0.0).astype(BF16)
    pre = pre_in + _mm(below, _b(jnp.broadcast_to(tot, (n_pages, PAGE_SIZE))))
    bias_ref[...] = jnp.where(key > thr, 0.0, jnp.where(eq, jnp.where(pre <= need, 0.0, NEG_SEL), NEG_SEL))
    n_eq = pre[n_pages - 1:n_pages, PAGE_SIZE - 1:PAGE_SIZE]
    sb = jnp.where(kself > thr, 0.0, jnp.where(kself == thr, jnp.where(n_eq + 1.0 <= need, 0.0, NEG_SEL), NEG_SEL))
    sbias_ref[...] = sb * jnp.ones((1, LANE), F32)


def _dsel(scores3, sself3):
    nb, n_pages, _ = scores3.shape
    ksel = min(DSA_TOPK, (n_pages * PAGE_SIZE + 1) // 4)
    return pl.pallas_call(
        functools.partial(_dsel_kernel, n_pages=n_pages, ksel=ksel),
        grid=(nb,),
        in_specs=[pl.BlockSpec((None, n_pages, PAGE_SIZE), lambda b: (b, 0, 0)),
                  pl.BlockSpec((None, 1, LANE), lambda b: (b, 0, 0)),
                  pl.BlockSpec((LANE, LANE), lambda b: (0, 0))],
        out_specs=[pl.BlockSpec((None, n_pages, PAGE_SIZE), lambda b: (b, 0, 0)),
                   pl.BlockSpec((None, 1, LANE), lambda b: (b, 0, 0))],
        out_shape=[jax.ShapeDtypeStruct((nb, n_pages, PAGE_SIZE), F32), jax.ShapeDtypeStruct((nb, 1, LANE), F32)],
        compiler_params=_cp(1),
        name="sample_select",
    )(scores3, sself3, _tri128())


def _datt_kernel(pt_ref, q_ref, bias_ref, sbias_ref, kself_ref, vself_ref, z_ref, *rest, ng):
    kpages = rest[:PAGES_PER_STEP]
    vpages = rest[PAGES_PER_STEP:2 * PAGES_PER_STEP]
    o_ref, m_s, l_s, acc_s = rest[2 * PAGES_PER_STEP:]
    g = pl.program_id(1)

    @pl.when(g == 0)
    def _():
        m_s[...] = jnp.full(m_s.shape, M_INIT, F32)
        l_s[...] = jnp.zeros(l_s.shape, F32)
        acc_s[...] = jnp.zeros(acc_s.shape, F32)

    sub = lax.broadcasted_iota(jnp.int32, (8, BRANCH_W), 0)
    own = sub == lax.broadcasted_iota(jnp.int32, (8, BRANCH_W), 1) // HEAD_DIM
    qbd = jnp.where(own, jnp.broadcast_to(q_ref[...], (8, BRANCH_W)), 0.0)
    q16 = _b(qbd)

    def update(lg, pv):
        m_prev = m_s[...]
        m_new = jnp.maximum(m_prev, jnp.max(lg, -1, keepdims=True))
        alpha = jnp.exp(m_prev - m_new)
        p = jnp.exp(lg - m_new)
        l_s[...] = l_s[...] * alpha + jnp.sum(p, -1, keepdims=True)
        acc_s[...] = acc_s[...] * alpha + pv(p)
        m_s[...] = m_new

    for j in range(PAGES_PER_STEP):
        lg = _nt(q16, _b(kpages[j][...])) * SCALE + bias_ref[j:j + 1, :]
        update(lg, lambda p, j=j: _mm(_b(p), _b(vpages[j][...])))

    @pl.when(g == ng - 1)
    def _():
        lg = jnp.sum(qbd * kself_ref[...], -1, keepdims=True) * SCALE + sbias_ref[:, 0:1]
        update(lg, lambda p: p * vself_ref[...])
        o8 = acc_s[...] / l_s[...]
        o = jnp.sum(jnp.where(own, o8, 0.0), 0, keepdims=True)
        o_ref[...] = o * _silu(z_ref[...])


def _datt(page_table, qb3, bias3, sbias3, kself3, vself3, z3, pool_k, pool_v):
    nb, n_pages = page_table.shape
    ng = n_pages // PAGES_PER_STEP
    row = lambda w: pl.BlockSpec((None, 1, w), lambda b, g, pt: (b, 0, 0))
    page = lambda j: pl.BlockSpec((None, PAGE_SIZE, BRANCH_W), lambda b, g, pt, j=j: (pt[b, g * PAGES_PER_STEP + j], 0, 0))
    return pl.pallas_call(
        functools.partial(_datt_kernel, ng=ng),
        grid_spec=pltpu.PrefetchScalarGridSpec(
            num_scalar_prefetch=1, grid=(nb, ng),
            in_specs=[row(BRANCH_W), pl.BlockSpec((None, PAGES_PER_STEP, PAGE_SIZE), lambda b, g, pt: (b, g, 0)),
                      row(LANE), row(BRANCH_W), row(BRANCH_W), row(BRANCH_W)]
            + [page(j) for j in range(PAGES_PER_STEP)] * 2,
            out_specs=row(BRANCH_W),
            scratch_shapes=[pltpu.VMEM((8, 1), F32), pltpu.VMEM((8, 1), F32), pltpu.VMEM((8, BRANCH_W), F32)]),
        out_shape=jax.ShapeDtypeStruct((nb, 1, BRANCH_W), F32),
        compiler_params=_cp(2),
        name="paged_attention",
    )(page_table, qb3, bias3, sbias3, kself3, vself3, z3, *([pool_k] * PAGES_PER_STEP), *([pool_v] * PAGES_PER_STEP))


def _sample_layer(x, x16, p, tabs, conv_buf, sg, sr, hl, mem_k, mem_v, pool_k, pool_v, pool_ik, page_table):
    nb = x.shape[0]
    hs = _matmul(x16, p["w_main"], F32, nb, 512)
    gs = _matmul(x16, p["w_gate"], BF16, nb, 1024)
    mk3 = mem_k.reshape(nb, -1, BRANCH_W)
    mv3 = mem_v.reshape(nb, -1, BRANCH_W)
    (y_a, y_c, y_d, y_e, sg_new, sr_new, hl_new, qb, kb, qi, ki, sself) = _decode(
        hs, conv_buf.transpose(1, 0, 2), p, sg, sr, hl, mk3, mv3, tabs)
    wcol = (hs[:, C_SMALL + 8:C_SMALL + 16] * IDX_SCALE)[:, :, None]
    scores = _didx(page_table, qi.reshape(nb, IDX_HEADS, LANE), wcol, pool_ik)
    n_pages = page_table.shape[1]
    bias3, sbias3 = _dsel(scores.reshape(nb, n_pages, PAGE_SIZE), sself[:, None, :])
    v_self = hs[:, C_DSA + 1024:C_DSA + 1536]
    y_b = _datt(page_table, qb[:, None, :], bias3, sbias3, kb[:, None, :], v_self[:, None, :],
                hs[:, None, C_DSA + 1536:C_DSA + 2048],
                pool_k.reshape(pool_k.shape[0], PAGE_SIZE, BRANCH_W), pool_v.reshape(pool_v.shape[0], PAGE_SIZE, BRANCH_W))
    merged = _merge([y_a, y_b.reshape(nb, BRANCH_W), y_c, y_d, y_e], gs, p["w_branch"], nb)
    y, y16 = _outnorm(merged, x.reshape(nb, D_MODEL), p["w_out"], p["ln_g"], p["ln_b"], nb)
    new_conv = jnp.concatenate([conv_buf[:, 1:], hs[:, None, C_CONV:C_CONV + CONV_DIM]], 1)
    return (y.reshape(nb, 1, D_MODEL), y16, (new_conv, sg_new, sr_new, hl_new),
            (kb.reshape(nb, 1, N_HEADS, HEAD_DIM), v_self.reshape(nb, 1, N_HEADS, HEAD_DIM), ki[:, None, :IDX_DIM]))


def kernel(x_prompt, x_sample, cache_conv, cache_dsa_k, cache_dsa_v, cache_dsa_idx_k, cache_mem_k, cache_mem_v,
           state_gdn, state_ret, state_lru, page_table, mem_prompt, w_in, conv_w, conv_b, gdn_a_log, gdn_dt_bias,
           gdn_norm_w, lru_lambda, lru_w_r, lru_b_r, lru_w_i, lru_b_i, w_mem_kv, w_branch, w_out, ln_g, ln_b):
    bsz, t, _ = x_prompt.shape
    nb = x_sample.shape[0]
    depth = w_in.shape[0]
    past = page_table.shape[1] * PAGE_SIZE
    tabs_p = _rope_tables(np.arange(t))
    tabs_s = _rope_tables(np.asarray([past]))
    mem16 = _b(mem_prompt.reshape(-1, D_MODEL))
    xp, xs = x_prompt, x_sample
    xp16, xs16 = _b(xp.reshape(-1, D_MODEL)), _b(xs.reshape(-1, D_MODEL))
    acc_p, acc_s = [], []
    for l in range(depth):
        p = _layer_params(l, w_in, conv_w, conv_b, gdn_a_log, gdn_dt_bias, gdn_norm_w, lru_lambda, lru_w_r, lru_b_r,
                          lru_w_i, lru_b_i, w_mem_kv, w_branch, w_out, ln_g, ln_b)
        xp, xp16, st, rows, mk, mv = _prompt_layer(xp, xp16, mem16, p, tabs_p)
        acc_p.append(st + rows + (mk, mv))
        xs, xs16, st, rows = _sample_layer(xs, xs16, p, tabs_s, cache_conv[l], state_gdn[l], state_ret[l], state_lru[l],
                                           cache_mem_k[l], cache_mem_v[l], cache_dsa_k[l], cache_dsa_v[l],
                                           cache_dsa_idx_k[l], page_table)
        acc_s.append(st + rows)
    stack = lambda acc, i: jnp.stack([a[i] for a in acc])
    return (xp, xs) + tuple(stack(acc_p, i) for i in range(9)) + tuple(stack(acc_s, i) for i in range(7))
```

```python
import functools
import math

import numpy as np
import jax
import jax.numpy as jnp
from jax import lax
from jax.experimental import pallas as pl
from jax.experimental.pallas import tpu as pltpu

F32 = jnp.float32
BF16 = jnp.bfloat16
HI = lax.Precision.HIGHEST

D_MODEL = 2048
HEAD_DIM = 128
BRANCH_W = 512
N_HEADS = 4
N_BRANCH = 5
CONV_W = 4
CONV_DIM = 2048
GDN_CHUNK = 64
GDN_GROUP = 256
RET_CHUNK = 256
DSA_TOPK = 256
IDX_HEADS = 8
IDX_DIM = 64
LRU_C = 8.0
ROPE_THETA = 10000.0
PAGE_SIZE = 128
LN_EPS = 1e-5
NORM_EPS = 1e-6
DEPTH = 2
DEEPNORM_ALPHA = (2 * DEPTH) ** 0.25
IN_SPLITS = (CONV_DIM, BRANCH_W, N_HEADS, N_HEADS,
             BRANCH_W, BRANCH_W, BRANCH_W, BRANCH_W, IDX_HEADS * IDX_DIM, IDX_DIM, IDX_HEADS,
             BRANCH_W, BRANCH_W, BRANCH_W, BRANCH_W,
             BRANCH_W,
             BRANCH_W, BRANCH_W,
             N_BRANCH * D_MODEL)

LANE = 128
C_CONV = 0
C_GDNZ = 2048
C_DSA = 2560
C_RET = 4608
C_LRUZ = 6656
C_MEM = 7168
C_QI = 8192
C_KI = 9216
C_SMALL = 9344
N_MAIN = 9728
N_GATE = N_BRANCH * D_MODEL
SCALE = HEAD_DIM ** -0.5
IDX_SCALE = (IDX_HEADS * IDX_DIM) ** -0.5
NEG = -1e30
VMEM_LIMIT = 56 * 1024 * 1024


def _cp(n_grid):
    return pltpu.CompilerParams(dimension_semantics=("arbitrary",) * n_grid, vmem_limit_bytes=VMEM_LIMIT)


def _b(x):
    return x.astype(BF16)


def _mm(a, b, precision=None):
    return jnp.dot(a, b, precision=precision, preferred_element_type=F32)


def _nt(a, b, precision=None):
    return lax.dot_general(a, b, (((1,), (1,)), ((), ())), precision=precision, preferred_element_type=F32)


def _tn(a, b, precision=None):
    return lax.dot_general(a, b, (((0,), (0,)), ((), ())), precision=precision, preferred_element_type=F32)


def _mm3(a, b):
    ah, bh = _b(a), _b(b)
    al, bl = _b(a - ah.astype(F32)), _b(b - bh.astype(F32))
    return _mm(ah, bh) + (_mm(ah, bl) + _mm(al, bh))


def _silu(x):
    return x * jax.nn.sigmoid(x)


def _rope128(x, cos, sin):
    return x * cos + pltpu.roll(x, 64, 1) * sin


def _rope_idx(x, cos, sina, sinb):
    return x * cos + pltpu.roll(x, 96, 1) * sina + pltpu.roll(x, 32, 1) * sinb


def _rope_tables(pos):
    pos = np.asarray(pos, np.float64)[:, None]
    ang = pos * (ROPE_THETA ** (-np.arange(64, dtype=np.float64) / 64))[None, :]
    c, s = np.cos(ang), np.sin(ang)
    cos128 = np.concatenate([c, c], 1)
    sin128 = np.concatenate([-s, s], 1)
    ang32 = pos * (ROPE_THETA ** (-np.arange(32, dtype=np.float64) / 32))[None, :]
    c, s = np.cos(ang32), np.sin(ang32)
    z = np.zeros_like(c)
    cos_i = np.concatenate([c, c, z, z], 1)
    sin_ia = np.concatenate([-s, z, z, z], 1)
    sin_ib = np.concatenate([z, s, z, z], 1)
    return tuple(jnp.asarray(a, F32) for a in (cos128, sin128, cos_i, sin_ia, sin_ib))


def _mm_kernel(x_ref, w_ref, o_ref):
    o_ref[...] = _mm(x_ref[...], w_ref[...]).astype(o_ref.dtype)


def _matmul(x, w, out_dtype, tm, tn):
    m, k = x.shape
    n = w.shape[1]
    return pl.pallas_call(
        _mm_kernel,
        grid=(m // tm, n // tn),
        in_specs=[pl.BlockSpec((tm, k), lambda i, j: (i, 0)),
                  pl.BlockSpec((k, tn), lambda i, j: (0, j))],
        out_specs=pl.BlockSpec((tm, tn), lambda i, j: (i, j)),
        out_shape=jax.ShapeDtypeStruct((m, n), out_dtype),
        compiler_params=_cp(2),
        name="proj_matmul",
    )(x, w)


W_PREP_ROWS = 128


def _relayout_kernel(w_ref, main_ref, gate_ref):
    offs = [int(o) for o in np.concatenate([[0], np.cumsum(IN_SPLITS)])]
    rows = main_ref.shape[0]

    def put(dst, src, n):
        main_ref[:, dst:dst + n] = _b(w_ref[:, src:src + n])

    def zero(dst, n):
        main_ref[:, dst:dst + n] = jnp.zeros((rows, n), BF16)

    put(C_CONV, offs[0], CONV_DIM)
    put(C_GDNZ, offs[1], BRANCH_W)
    put(C_DSA, offs[4], 4 * BRANCH_W)
    put(C_RET, offs[11], 4 * BRANCH_W)
    put(C_LRUZ, offs[15], BRANCH_W)
    put(C_MEM, offs[16], 2 * BRANCH_W)
    for h in range(IDX_HEADS):
        put(C_QI + h * LANE, offs[8] + h * IDX_DIM, IDX_DIM)
        zero(C_QI + h * LANE + IDX_DIM, LANE - IDX_DIM)
    put(C_KI, offs[9], IDX_DIM)
    zero(C_KI + IDX_DIM, LANE - IDX_DIM)
    put(C_SMALL, offs[2], 2 * N_HEADS)
    put(C_SMALL + 2 * N_HEADS, offs[10], IDX_HEADS)
    zero(C_SMALL + 2 * N_HEADS + IDX_HEADS, N_MAIN - C_SMALL - 2 * N_HEADS - IDX_HEADS)
    gate_ref[...] = _b(w_ref[:, offs[18]:offs[18] + N_GATE])


def _prep_w_in(w_in, l):
    _, d, n_in = w_in.shape
    return pl.pallas_call(
        _relayout_kernel,
        grid=(d // W_PREP_ROWS,),
        in_specs=[pl.BlockSpec((None, W_PREP_ROWS, n_in), lambda i: (l, i, 0))],
        out_specs=[pl.BlockSpec((W_PREP_ROWS, N_MAIN), lambda i: (i, 0)),
                   pl.BlockSpec((W_PREP_ROWS, N_GATE), lambda i: (i, 0))],
        out_shape=[jax.ShapeDtypeStruct((d, N_MAIN), BF16), jax.ShapeDtypeStruct((d, N_GATE), BF16)],
        compiler_params=_cp(1),
        name="w_in_relayout",
    )(w_in)


def _ropek_kernel(k_ref, v_ref, ki_ref, cos_ref, sin_ref, ci_ref, sia_ref, sib_ref,
                  kb_ref, k16_ref, v16_ref, kif_ref, ki16_ref):
    cos, sin = cos_ref[...], sin_ref[...]
    for h in range(N_HEADS):
        sl = slice(h * HEAD_DIM, (h + 1) * HEAD_DIM)
        r = _rope128(k_ref[:, sl], cos, sin)
        kb_ref[:, sl] = r
        k16_ref[:, sl] = _b(r)
    v16_ref[...] = _b(v_ref[...])
    ki = _rope_idx(ki_ref[...], ci_ref[...], sia_ref[...], sib_ref[...])
    kif_ref[...] = ki
    ki16_ref[...] = _b(ki)


def _rope_k(h3, tabs, tr):
    bsz, t, _ = h3.shape
    cb = lambda c, w: c // w
    tab = pl.BlockSpec((tr, LANE), lambda b, i: (i, 0))
    row = lambda w, c: pl.BlockSpec((None, tr, w), lambda b, i: (b, i, c))
    return pl.pallas_call(
        _ropek_kernel,
        grid=(bsz, t // tr),
        in_specs=[row(512, cb(C_DSA + 512, 512)), row(512, cb(C_DSA + 1024, 512)), row(LANE, cb(C_KI, LANE)),
                  tab, tab, tab, tab, tab],
        out_specs=[row(512, 0), row(512, 0), row(512, 0), row(LANE, 0), row(LANE, 0)],
        out_shape=[jax.ShapeDtypeStruct((bsz, t, 512), F32), jax.ShapeDtypeStruct((bsz, t, 512), BF16),
                   jax.ShapeDtypeStruct((bsz, t, 512), BF16), jax.ShapeDtypeStruct((bsz, t, LANE), F32),
                   jax.ShapeDtypeStruct((bsz, t, LANE), BF16)],
        compiler_params=_cp(2),
        name="rope_k",
    )(h3, h3, h3, *tabs)


def _gdn_kernel(q_ref, k_ref, v_ref, z_ref, sm_ref, wq_ref, wk_ref, wv_ref, bq_ref, bk_ref, bv_ref,
                hq_ref, hk_ref, hv_ref, par_ref, s0_ref, y_ref, sfin_ref,
                scq, sck, scv, u_s, w_s, qk_s, qd_s, kt_s, gl_s, *, t, c, gr):
    h = pl.program_id(1)
    nc = t // c
    cpg = gr // c
    for src, hist, scr in ((q_ref, hq_ref, scq), (k_ref, hk_ref, sck), (v_ref, hv_ref, scv)):
        scr[pl.ds(0, 8), :] = hist[...]
        scr[pl.ds(8, t), :] = src[...]
    ii = lax.broadcasted_iota(jnp.int32, (gr, gr), 0)
    jj = lax.broadcasted_iota(jnp.int32, (gr, gr), 1)
    same = (ii // c) == (jj // c)
    tril = jnp.logical_and(same, ii >= jj)
    strict = jnp.logical_and(same, ii > jj)
    eye = jnp.where(ii == jj, 1.0, 0.0).astype(F32)
    trilf = jnp.where(tril, 1.0, 0.0).astype(F32)
    samef = jnp.where(same, 1.0, 0.0).astype(F32)
    lane = lax.broadcasted_iota(jnp.int32, (gr, LANE), 1)
    e0 = jnp.where(lane == 0, 1.0, 0.0).astype(F32)
    neg_a = -jnp.exp(par_ref[0:1, 0:1])
    dt_bias = par_ref[1:2, 0:1]
    norm_w = par_ref[2:3, :]

    def conv(scr, w_ref, b_ref, r0):
        acc = b_ref[...] + scr[pl.ds(r0 + 5, gr), :] * w_ref[0:1, :]
        for j in range(1, CONV_W):
            acc = acc + scr[pl.ds(r0 + 5 + j, gr), :] * w_ref[j:j + 1, :]
        return acc

    def prep(gi, carry):
        r0 = pl.multiple_of(gi * gr, gr)
        qc = _silu(conv(scq, wq_ref, bq_ref, r0))
        kc = _silu(conv(sck, wk_ref, bk_ref, r0))
        vc = _silu(conv(scv, wv_ref, bv_ref, r0))
        qn = qc * lax.rsqrt(jnp.sum(qc * qc, -1, keepdims=True) + NORM_EPS) * SCALE
        kn = kc * lax.rsqrt(jnp.sum(kc * kc, -1, keepdims=True) + NORM_EPS)
        sm = sm_ref[pl.ds(r0, gr), :]
        a_col = jnp.sum(jnp.where(lane == h, sm, 0.0), -1, keepdims=True)
        b_col = jnp.sum(jnp.where(lane == N_HEADS + h, sm, 0.0), -1, keepdims=True)
        beta = jax.nn.sigmoid(b_col)
        gb = jnp.broadcast_to(neg_a * jax.nn.softplus(a_col + dt_bias), (gr, LANE))
        gcb = _mm(trilf, gb, HI)
        glb = _mm(samef, gb, HI)
        gc = gcb[:, 0:1]
        gcr = _nt(e0, gcb, HI)
        decay = jnp.where(tril, jnp.exp(jnp.minimum(gc - gcr, 0.0)), 0.0)
        q16, k16 = _b(qn), _b(kn)
        a = jnp.where(strict, beta * _nt(k16, k16) * decay, 0.0)
        x = -a
        p = eye + x
        for _ in range(int(math.log2(c)) - 1):
            x = _mm3(x, x)
            p = _mm3(p, eye + x)
        egc = jnp.exp(gc)
        u = _mm3(p, vc * beta)
        w = _mm3(p, kn * (beta * egc))
        qk = _nt(q16, k16) * decay
        qd = qn * egc
        kt = kn * jnp.exp(glb[:, 0:1] - gc)
        egl = jnp.exp(glb)
        for j in range(cpg):
            ci = gi * cpg + j
            rows = slice(j * c, (j + 1) * c)
            u_s[ci] = u[rows]
            w_s[ci] = _b(w[rows])
            qk_s[ci] = _b(qk[rows, j * c:(j + 1) * c])
            qd_s[ci] = _b(qd[rows])
            kt_s[ci] = kt[rows]
            gl_s[ci] = egl[j * c:j * c + 1]
        return carry

    lax.fori_loop(0, t // gr, prep, 0, unroll=math.gcd(t // gr, 2))

    def step(ci, s):
        r0 = pl.multiple_of(ci * c, c)
        s16 = _b(s)
        v_new = u_s[ci] - _mm(w_s[ci], s16)
        o = _mm(qd_s[ci], s16) + _mm(qk_s[ci], _b(v_new))
        s = s * gl_s[ci] + _tn(kt_s[ci], v_new)
        o = o * lax.rsqrt(jnp.mean(o * o, -1, keepdims=True) + NORM_EPS) * norm_w
        y_ref[pl.ds(r0, c), :] = o * _silu(z_ref[pl.ds(r0, c), :])
        return s

    sfin_ref[...] = lax.fori_loop(0, nc, step, s0_ref[...])


def _gdn(h3, hist8, conv_w, conv_b, par, s0):
    bsz, t, _ = h3.shape
    c = math.gcd(t, GDN_CHUNK)
    gr = math.gcd(t, GDN_GROUP)
    nc = t // c
    col = lambda blk: pl.BlockSpec((None, t, LANE), lambda b, h, blk=blk: (b, 0, blk + h))
    wsp = lambda blk: pl.BlockSpec((CONV_W, LANE), lambda b, h, blk=blk: (0, blk + h))
    bsp = lambda blk: pl.BlockSpec((1, LANE), lambda b, h, blk=blk: (0, blk + h))
    hsp = lambda blk: pl.BlockSpec((None, 8, LANE), lambda b, h, blk=blk: (b, 0, blk + h))
    st = pl.BlockSpec((None, None, HEAD_DIM, HEAD_DIM), lambda b, h: (b, h, 0, 0))
    return pl.pallas_call(
        functools.partial(_gdn_kernel, t=t, c=c, gr=gr),
        grid=(bsz, N_HEADS),
        in_specs=[col(0), col(4), col(8), col(C_GDNZ // LANE),
                  pl.BlockSpec((None, t, LANE), lambda b, h: (b, 0, C_SMALL // LANE)),
                  wsp(0), wsp(4), wsp(8), bsp(0), bsp(4), bsp(8), hsp(0), hsp(4), hsp(8),
                  pl.BlockSpec((None, 8, LANE), lambda b, h: (h, 0, 0)), st],
        out_specs=[pl.BlockSpec((None, t, LANE), lambda b, h: (b, 0, h)), st],
        out_shape=[jax.ShapeDtypeStruct((bsz, t, BRANCH_W), F32),
                   jax.ShapeDtypeStruct((bsz, N_HEADS, HEAD_DIM, HEAD_DIM), F32)],
        scratch_shapes=[pltpu.VMEM((t + 8, LANE), F32)] * 3 + [
            pltpu.VMEM((nc, c, LANE), F32), pltpu.VMEM((nc, c, LANE), BF16), pltpu.VMEM((nc, c, c), BF16),
            pltpu.VMEM((nc, c, LANE), BF16), pltpu.VMEM((nc, c, LANE), F32), pltpu.VMEM((nc, 1, LANE), F32)],
        compiler_params=_cp(2),
        name="gdn",
    )(h3, h3, h3, h3, h3, conv_w, conv_w, conv_w, conv_b, conv_b, conv_b, hist8, hist8, hist8, par, s0)


def _ret_kernel(q_ref, k_ref, v_ref, z_ref, cos_ref, sin_ref, par_ref, r0_ref, y_ref, rfin_ref, *, t, c):
    nc = t // c
    lg = jnp.log(par_ref[0:1, 0:1])
    ii = lax.broadcasted_iota(jnp.int32, (c, c), 0)
    jj = lax.broadcasted_iota(jnp.int32, (c, c), 1)
    dmat = jnp.where(ii >= jj, jnp.exp((ii - jj).astype(F32) * lg), 0.0)
    icol = lax.broadcasted_iota(jnp.int32, (c, 1), 0).astype(F32)
    xi = jnp.exp((icol + 1.0) * lg)
    zeta = jnp.exp((c - 1.0 - icol) * lg)
    g_c = jnp.exp(c * lg)

    def step(ci, r):
        rows = pl.ds(pl.multiple_of(ci * c, c), c)
        cos, sin = cos_ref[rows, :], sin_ref[rows, :]
        q = _rope128(q_ref[rows, :], cos, sin)
        k = _rope128(k_ref[rows, :], cos, sin) * SCALE
        v = v_ref[rows, :]
        q16, k16, v16 = _b(q), _b(k), _b(v)
        o = _mm(_b(_nt(q16, k16) * dmat), v16) + _mm(_b(q * xi), _b(r))
        r = r * g_c + _tn(k * zeta, v)
        mu = jnp.mean(o, -1, keepdims=True)
        d = o - mu
        var = jnp.mean(d * d, -1, keepdims=True)
        y_ref[rows, :] = d * lax.rsqrt(var + LN_EPS) * _silu(z_ref[rows, :])
        return r

    rfin_ref[...] = lax.fori_loop(0, nc, step, r0_ref[...], unroll=math.gcd(nc, CHUNK_UNROLL))


def _ret_par():
    gamma = (1.0 - 2.0 ** (-5.0 - np.arange(N_HEADS, dtype=np.float64))).astype(np.float32)
    return jnp.asarray(np.broadcast_to(gamma[:, None, None], (N_HEADS, 8, LANE)).copy(), F32)


def _ret(h3, cos, sin, par, r0):
    bsz, t, _ = h3.shape
    c = math.gcd(t, RET_CHUNK)
    col = lambda blk: pl.BlockSpec((None, t, LANE), lambda b, h, blk=blk: (b, 0, blk + h))
    tab = pl.BlockSpec((t, LANE), lambda b, h: (0, 0))
    st = pl.BlockSpec((None, None, HEAD_DIM, HEAD_DIM), lambda b, h: (b, h, 0, 0))
    base = C_RET // LANE
    return pl.pallas_call(
        functools.partial(_ret_kernel, t=t, c=c),
        grid=(bsz, N_HEADS),
        in_specs=[col(base), col(base + 4), col(base + 8), col(base + 12), tab, tab,
                  pl.BlockSpec((None, 8, LANE), lambda b, h: (h, 0, 0)), st],
        out_specs=[pl.BlockSpec((None, t, LANE), lambda b, h: (b, 0, h)), st],
        out_shape=[jax.ShapeDtypeStruct((bsz, t, BRANCH_W), F32),
                   jax.ShapeDtypeStruct((bsz, N_HEADS, HEAD_DIM, HEAD_DIM), F32)],
        compiler_params=_cp(2),
        name="retention",
    )(h3, h3, h3, h3, cos, sin, par, r0)


def _lru_gates(x, wr_ref, wi_ref, br, bi, lam):
    rs, is_ = [], []
    for h in range(N_HEADS):
        sl = slice(h * HEAD_DIM, (h + 1) * HEAD_DIM)
        xh = _b(x[:, sl])
        rs.append(_mm(xh, wr_ref[h]))
        is_.append(_mm(xh, wi_ref[h]))
    r = jax.nn.sigmoid(jnp.concatenate(rs, 1) + br)
    ig = jax.nn.sigmoid(jnp.concatenate(is_, 1) + bi)
    log_a = -LRU_C * r * jax.nn.softplus(-lam)
    a = jnp.exp(log_a)
    th = jnp.tanh(log_a)
    one_m_a2 = -2.0 * th / (1.0 - th)
    return a, jnp.sqrt(one_m_a2) * (ig * x)


def _lru_kernel(x_ref, z_ref, w_ref, b_ref, hist_ref, wr_ref, wi_ref, br_ref, bi_ref, lam_ref, h0_ref,
                y_ref, hfin_ref, scx, a_s, b_s, *, t, rb):
    for h in range(N_HEADS):
        sl = slice(h * HEAD_DIM, (h + 1) * HEAD_DIM)
        scx[h, pl.ds(0, 8), :] = hist_ref[:, sl]
        scx[h, pl.ds(8, t), :] = x_ref[:, sl]
    br, bi, lam = br_ref[...], bi_ref[...], lam_ref[...]

    def gates(i, carry):
        r0 = pl.multiple_of(i * rb, rb)
        xs = []
        for h in range(N_HEADS):
            sl = slice(h * HEAD_DIM, (h + 1) * HEAD_DIM)
            xh = b_ref[:, sl] + scx[h, pl.ds(r0 + 5, rb), :] * w_ref[0:1, sl]
            for j in range(1, CONV_W):
                xh = xh + scx[h, pl.ds(r0 + 5 + j, rb), :] * w_ref[j:j + 1, sl]
            xs.append(xh)
        x = jnp.concatenate(xs, 1)
        a, b = _lru_gates(x, wr_ref, wi_ref, br, bi, lam)
        a_s[pl.ds(r0, rb), :] = a
        b_s[pl.ds(r0, rb), :] = b
        return carry

    lax.fori_loop(0, t // rb, gates, 0)

    def scan(i, h):
        h = a_s[pl.ds(i, 1), :] * h + b_s[pl.ds(i, 1), :]
        b_s[pl.ds(i, 1), :] = h
        return h

    hfin_ref[...] = lax.fori_loop(0, t, scan, h0_ref[...], unroll=8)

    def gate_out(i, carry):
        rows = pl.ds(pl.multiple_of(i * rb, rb), rb)
        y_ref[rows, :] = b_s[rows, :] * _silu(z_ref[rows, :])
        return carry

    lax.fori_loop(0, t // rb, gate_out, 0)


def _lru(h3, hist8, conv_w, conv_b, w_r, w_i, b_r, b_i, lam, h0):
    bsz, t, _ = h3.shape
    rb = math.gcd(t, 256)
    xblk = (C_CONV + 3 * BRANCH_W) // BRANCH_W
    vec = pl.BlockSpec((1, BRANCH_W), lambda b: (0, 0))
    wsp = pl.BlockSpec((N_HEADS, HEAD_DIM, HEAD_DIM), lambda b: (0, 0, 0))
    return pl.pallas_call(
        functools.partial(_lru_kernel, t=t, rb=rb),
        grid=(bsz,),
        in_specs=[pl.BlockSpec((None, t, BRANCH_W), lambda b: (b, 0, xblk)),
                  pl.BlockSpec((None, t, BRANCH_W), lambda b: (b, 0, C_LRUZ // BRANCH_W)),
                  pl.BlockSpec((CONV_W, BRANCH_W), lambda b: (0, xblk)),
                  pl.BlockSpec((1, BRANCH_W), lambda b: (0, xblk)),
                  pl.BlockSpec((None, 8, BRANCH_W), lambda b: (b, 0, xblk)),
                  wsp, wsp, vec, vec, vec,
                  pl.BlockSpec((None, 1, BRANCH_W), lambda b: (b, 0, 0))],
        out_specs=[pl.BlockSpec((None, t, BRANCH_W), lambda b: (b, 0, 0)),
                   pl.BlockSpec((None, 1, BRANCH_W), lambda b: (b, 0, 0))],
        out_shape=[jax.ShapeDtypeStruct((bsz, t, BRANCH_W), F32), jax.ShapeDtypeStruct((bsz, 1, BRANCH_W), F32)],
        scratch_shapes=[pltpu.VMEM((N_HEADS, t + 8, HEAD_DIM), F32), pltpu.VMEM((t, BRANCH_W), F32),
                        pltpu.VMEM((t, BRANCH_W), F32)],
        compiler_params=_cp(1),
        name="rglru",
    )(h3, h3, conv_w, conv_b, hist8, w_r, w_i, b_r, b_i, lam, h0)


CHUNK_UNROLL = 4
BISECT_STEPS = 18


def _kth_threshold(count_ge, count_gt, next_above, lo, hi, k):
    c_hi = count_ge(hi)
    top = c_hi >= k
    lo = jnp.where(top, hi, lo)

    def halve(_, c):
        lo, hi = c
        mid = lo + (hi - lo) * 0.5
        up = count_ge(mid) >= k
        return jnp.where(up, mid, lo), jnp.where(up, hi, mid)

    lo, _ = lax.fori_loop(0, BISECT_STEPS, halve, (lo, hi))

    def more(c):
        return jnp.max(c[1] - k) > 0.5

    def walk(c):
        thr, c_gt = c
        nxt = jnp.where(c_gt > k, next_above(thr), thr)
        return nxt, count_gt(nxt)

    return lax.while_loop(more, walk, (lo, count_gt(lo)))


def _dsa_kernel(q_ref, z_ref, qi_ref, sm_ref, cos_ref, sin_ref, ci_ref, sia_ref, sib_ref,
                k_ref, v_ref, ki_ref, tri_ref, o_ref, sc_s, bias_s, *, tq, t, ksel):
    i = pl.program_id(1)
    ki16 = ki_ref[...]
    sm = sm_ref[...]
    ci, sia, sib = ci_ref[...], sia_ref[...], sib_ref[...]
    score = jnp.zeros((tq, t), F32)
    for h in range(IDX_HEADS):
        x = _rope_idx(qi_ref[:, h * LANE:(h + 1) * LANE], ci, sia, sib)
        s = _nt(_b(x), ki16)
        w = sm[:, 8 + h:9 + h] * IDX_SCALE
        score = score + jnp.maximum(s, 0.0) * w
    qpos = i * tq + lax.broadcasted_iota(jnp.int32, (tq, 1), 0)
    kpos = lax.broadcasted_iota(jnp.int32, (1, t), 1)
    valid = kpos <= qpos
    sc_s[...] = jnp.where(valid, score, -jnp.inf)
    k_row = jnp.minimum(float(ksel), (qpos + 1).astype(F32))

    def count_ge(th):
        return jnp.sum(jnp.where(sc_s[...] >= th, 1.0, 0.0), -1, keepdims=True)

    def count_gt(th):
        return jnp.sum(jnp.where(sc_s[...] > th, 1.0, 0.0), -1, keepdims=True)

    def next_above(th):
        s = sc_s[...]
        return jnp.min(jnp.where(s > th, s, jnp.inf), -1, keepdims=True)

    lo = jnp.min(jnp.where(valid, score, jnp.inf), -1, keepdims=True)
    hi = jnp.max(sc_s[...], -1, keepdims=True)
    thr, n_gt = _kth_threshold(count_ge, count_gt, next_above, lo, hi, k_row)
    need = k_row - n_gt
    run = jnp.zeros((tq, 1), F32)
    tri = tri_ref[...]
    for cb in range(t // LANE):
        sl = slice(cb * LANE, (cb + 1) * LANE)
        sc = sc_s[:, sl]
        eq = sc == thr
        pre = _mm(jnp.where(eq, 1.0, 0.0).astype(BF16), tri) + run
        bias_s[:, sl] = jnp.where(sc > thr, 0.0, jnp.where(eq, jnp.where(pre <= need, 0.0, NEG), NEG))
        run = pre[:, LANE - 1:LANE]
    bias = bias_s[...]
    cos, sin = cos_ref[...], sin_ref[...]
    for h in range(N_HEADS):
        sl = slice(h * HEAD_DIM, (h + 1) * HEAD_DIM)
        q = _rope128(q_ref[:, sl], cos, sin)
        lg = _nt(_b(q), k_ref[:, sl]) * SCALE + bias
        m = jnp.max(lg, -1, keepdims=True)
        p = jnp.exp(lg - m)
        l = jnp.sum(p, -1, keepdims=True)
        o = _mm(_b(p), v_ref[:, sl]) / l
        o_ref[:, sl] = o * _silu(z_ref[:, sl])


def _tri128():
    return jnp.asarray(np.triu(np.ones((LANE, LANE), np.float32)), BF16)


def _dsa(h3, k16, v16, ki16, tabs, tq):
    bsz, t, _ = h3.shape
    ksel = min(DSA_TOPK, t // 4)
    cos, sin, ci, sia, sib = tabs
    qtab = pl.BlockSpec((tq, LANE), lambda b, i: (i, 0))
    full = lambda w: pl.BlockSpec((None, t, w), lambda b, i: (b, 0, 0))
    return pl.pallas_call(
        functools.partial(_dsa_kernel, tq=tq, t=t, ksel=ksel),
        grid=(bsz, t // tq),
        in_specs=[pl.BlockSpec((None, tq, 512), lambda b, i: (b, i, C_DSA // 512)),
                  pl.BlockSpec((None, tq, 512), lambda b, i: (b, i, (C_DSA + 1536) // 512)),
                  pl.BlockSpec((None, tq, 1024), lambda b, i: (b, i, C_QI // 1024)),
                  pl.BlockSpec((None, tq, LANE), lambda b, i: (b, i, C_SMALL // LANE)),
                  qtab, qtab, qtab, qtab, qtab,
                  full(512), full(512), full(LANE),
                  pl.BlockSpec((LANE, LANE), lambda b, i: (0, 0))],
        out_specs=pl.BlockSpec((None, tq, 512), lambda b, i: (b, i, 0)),
        out_shape=jax.ShapeDtypeStruct((bsz, t, BRANCH_W), F32),
        scratch_shapes=[pltpu.VMEM((tq, t), F32), pltpu.VMEM((tq, t), F32)],
        compiler_params=_cp(2),
        name="dsa_prompt",
    )(h3, h3, h3, h3, cos, sin, ci, sia, sib, k16, v16, ki16, _tri128())


def _mem_kernel(q_ref, z_ref, mk_ref, mv_ref, o_ref):
    for h in range(N_HEADS):
        sl = slice(h * HEAD_DIM, (h + 1) * HEAD_DIM)
        lg = _nt(_b(q_ref[:, sl]), _b(mk_ref[:, sl])) * SCALE
        m = jnp.max(lg, -1, keepdims=True)
        p = jnp.exp(lg - m)
        l = jnp.sum(p, -1, keepdims=True)
        o = _mm(_b(p), _b(mv_ref[:, sl])) / l
        o_ref[:, sl] = o * _silu(z_ref[:, sl])


def _mem(h3, mkv3, tq):
    bsz, t, _ = h3.shape
    nm = mkv3.shape[1]
    return pl.pallas_call(
        _mem_kernel,
        grid=(bsz, t // tq),
        in_specs=[pl.BlockSpec((None, tq, 512), lambda b, i: (b, i, C_MEM // 512)),
                  pl.BlockSpec((None, tq, 512), lambda b, i: (b, i, C_MEM // 512 + 1)),
                  pl.BlockSpec((None, nm, 512), lambda b, i: (b, 0, 0)),
                  pl.BlockSpec((None, nm, 512), lambda b, i: (b, 0, 1))],
        out_specs=pl.BlockSpec((None, tq, 512), lambda b, i: (b, i, 0)),
        out_shape=jax.ShapeDtypeStruct((bsz, t, BRANCH_W), F32),
        compiler_params=_cp(2),
        name="mem_attn",
    )(h3, h3, mkv3, mkv3)


def _merge_kernel(ya_ref, yb_ref, yc_ref, yd_ref, ye_ref, g_ref, w_ref, o_ref):
    acc = None
    for nbr, y_ref in enumerate((ya_ref, yb_ref, yc_ref, yd_ref, ye_ref)):
        p = _mm(_b(y_ref[...]), w_ref[nbr])
        g = jax.nn.sigmoid(g_ref[:, nbr * D_MODEL:(nbr + 1) * D_MODEL].astype(F32))
        acc = g * p if acc is None else acc + g * p
    o_ref[...] = _b(acc)


def _merge(ys, g, w_br, tm):
    m = g.shape[0]
    ysp = pl.BlockSpec((tm, BRANCH_W), lambda i: (i, 0))
    return pl.pallas_call(
        _merge_kernel,
        grid=(m // tm,),
        in_specs=[ysp] * N_BRANCH + [pl.BlockSpec((tm, N_GATE), lambda i: (i, 0)),
                                     pl.BlockSpec((N_BRANCH, BRANCH_W, D_MODEL), lambda i: (0, 0, 0))],
        out_specs=pl.BlockSpec((tm, D_MODEL), lambda i: (i, 0)),
        out_shape=jax.ShapeDtypeStruct((m, D_MODEL), BF16),
        compiler_params=_cp(1),
        name="gated_merge",
    )(*ys, g, w_br)


def _outnorm_kernel(m_ref, x_ref, w_ref, g_ref, b_ref, y_ref, y16_ref):
    v = DEEPNORM_ALPHA * x_ref[...] + _mm(m_ref[...], w_ref[...])
    mu = jnp.mean(v, -1, keepdims=True)
    d = v - mu
    var = jnp.mean(d * d, -1, keepdims=True)
    y = d * lax.rsqrt(var + LN_EPS) * g_ref[...] + b_ref[...]
    y_ref[...] = y
    y16_ref[...] = _b(y)


def _outnorm(merged, x, w_out, ln_g, ln_b, tm):
    m = x.shape[0]
    row = pl.BlockSpec((tm, D_MODEL), lambda i: (i, 0))
    vec = pl.BlockSpec((1, D_MODEL), lambda i: (0, 0))
    return pl.pallas_call(
        _outnorm_kernel,
        grid=(m // tm,),
        in_specs=[row, row, pl.BlockSpec((D_MODEL, D_MODEL), lambda i: (0, 0)), vec, vec],
        out_specs=[row, row],
        out_shape=[jax.ShapeDtypeStruct((m, D_MODEL), F32), jax.ShapeDtypeStruct((m, D_MODEL), BF16)],
        compiler_params=_cp(1),
        name="out_norm",
    )(merged, x, w_out, ln_g, ln_b)


def _layer_params(l, w_in, conv_w, conv_b, gdn_a_log, gdn_dt_bias, gdn_norm_w, lru_lambda, lru_w_r, lru_b_r,
                  lru_w_i, lru_b_i, w_mem_kv, w_branch, w_out, ln_g, ln_b):
    w_main, w_gate = _prep_w_in(w_in, l)
    rows = jnp.stack([jnp.broadcast_to(gdn_a_log[l][:, None], (N_HEADS, LANE)),
                      jnp.broadcast_to(gdn_dt_bias[l][:, None], (N_HEADS, LANE)),
                      jnp.broadcast_to(gdn_norm_w[l][None, :], (N_HEADS, LANE))], 1)
    gdn_par = jnp.concatenate([rows, jnp.zeros((N_HEADS, 5, LANE), F32)], 1)
    return dict(
        w_main=w_main, w_gate=w_gate, conv_w=conv_w[l], conv_b=conv_b[l][None, :], gdn_par=gdn_par,
        ret_par=_ret_par(), lru_w_r=_b(lru_w_r[l]), lru_w_i=_b(lru_w_i[l]),
        lru_b_r=lru_b_r[l][None, :], lru_b_i=lru_b_i[l][None, :], lam=lru_lambda[l][None, :],
        w_mem_kv=_b(w_mem_kv[l]), w_branch=_b(w_branch[l]), w_out=_b(w_out[l]),
        ln_g=ln_g[l][None, :], ln_b=ln_b[l][None, :])


def _hist8(buf):
    return jnp.concatenate([jnp.zeros((buf.shape[0], 8 - (CONV_W - 1), buf.shape[2]), buf.dtype), buf], 1)


def _prompt_layer(x, x16, mem16, p, tabs):
    bsz, t, _ = x.shape
    m = bsz * t
    tm = math.gcd(m, 1024)
    h = _matmul(x16, p["w_main"], F32, tm, 512)
    g = _matmul(x16, p["w_gate"], BF16, tm, 1024)
    mkv = _matmul(mem16, p["w_mem_kv"], F32, math.gcd(mem16.shape[0], 512), 512)
    h3 = h.reshape(bsz, t, N_MAIN)
    mkv3 = mkv.reshape(bsz, -1, 2 * BRANCH_W)
    zero_hist = jnp.zeros((bsz, 8, CONV_DIM), F32)
    zero_state = jnp.zeros((bsz, N_HEADS, HEAD_DIM, HEAD_DIM), F32)
    cos, sin = tabs[0], tabs[1]
    y_a, s_gdn = _gdn(h3, zero_hist, p["conv_w"], p["conv_b"], p["gdn_par"], zero_state)
    kb, k16, v16, kif, ki16 = _rope_k(h3, tabs, math.gcd(t, 512))
    y_b = _dsa(h3, k16, v16, ki16, tabs, math.gcd(t, 256))
    y_c, s_ret = _ret(h3, cos, sin, p["ret_par"], zero_state)
    y_d, s_lru = _lru(h3, zero_hist, p["conv_w"], p["conv_b"], p["lru_w_r"], p["lru_w_i"], p["lru_b_r"],
                      p["lru_b_i"], p["lam"], jnp.zeros((bsz, 1, BRANCH_W), F32))
    y_e = _mem(h3, mkv3, math.gcd(t, 512))
    ys = [a.reshape(m, BRANCH_W) for a in (y_a, y_b, y_c, y_d, y_e)]
    merged = _merge(ys, g, p["w_branch"], math.gcd(m, 256))
    y, y16 = _outnorm(merged, x.reshape(m, D_MODEL), p["w_out"], p["ln_g"], p["ln_b"], math.gcd(m, 512))
    new_conv = h3[:, t - (CONV_W - 1):, C_CONV:C_CONV + CONV_DIM]
    v_rows = h3[:, :, C_DSA + 1024:C_DSA + 1536].reshape(bsz, t, N_HEADS, HEAD_DIM)
    mk = mkv3[:, :, :BRANCH_W].reshape(bsz, -1, N_HEADS, HEAD_DIM)
    mv = mkv3[:, :, BRANCH_W:].reshape(bsz, -1, N_HEADS, HEAD_DIM)
    return (y.reshape(bsz, t, D_MODEL), y16,
            (new_conv, s_gdn, s_ret, s_lru.reshape(bsz, BRANCH_W)),
            (kb.reshape(bsz, t, N_HEADS, HEAD_DIM), v_rows, kif[:, :, :IDX_DIM]), mk, mv)


def _dec_kernel(hs_ref, hist_ref, cw_ref, cb_ref, gpar_ref, rpar_ref, sg_ref, sr_ref, hl_ref,
                wr_ref, wi_ref, br_ref, bi_ref, lam_ref, mk_ref, mv_ref,
                cos_ref, sin_ref, ci_ref, sia_ref, sib_ref,
                ya_ref, yc_ref, yd_ref, ye_ref, sgo_ref, sro_ref, hlo_ref, qb_ref, kb_ref, qi_ref, ki_ref, ss_ref,
                wk_s, vb_s, qd_s, qkg_s, eg_s, kg_s, qx_s, qkr_s, kr_s, vr_s, mq_s, oa_s, oc_s, oe_s, *, nb):
    hd = HEAD_DIM
    hsl = lambda h: slice(h * hd, (h + 1) * hd)
    x_in = hs_ref[:, C_CONV:C_CONV + CONV_DIM]
    conv = cb_ref[...] + x_in * cw_ref[CONV_W - 1:CONV_W, :]
    for j in range(CONV_W - 1):
        conv = conv + hist_ref[j] * cw_ref[j:j + 1, :]
    sm = hs_ref[:, C_SMALL:C_SMALL + LANE]
    cos, sin = cos_ref[...], sin_ref[...]
    ci, sia, sib = ci_ref[...], sia_ref[...], sib_ref[...]
    ones = jnp.ones((1, hd), F32)
    for h in range(N_HEADS):
        sl = hsl(h)
        q = _silu(conv[:, h * hd:(h + 1) * hd])
        k = _silu(conv[:, BRANCH_W + h * hd:BRANCH_W + (h + 1) * hd])
        v = _silu(conv[:, 2 * BRANCH_W + h * hd:2 * BRANCH_W + (h + 1) * hd])
        q = q * lax.rsqrt(jnp.sum(q * q, -1, keepdims=True) + NORM_EPS) * SCALE
        k = k * lax.rsqrt(jnp.sum(k * k, -1, keepdims=True) + NORM_EPS)
        beta = jax.nn.sigmoid(sm[:, N_HEADS + h:N_HEADS + h + 1])
        g = -jnp.exp(gpar_ref[h, 0:1, 0:1]) * jax.nn.softplus(sm[:, h:h + 1] + gpar_ref[h, 1:2, 0:1])
        eg = jnp.exp(g)
        wk_s[h] = k * (beta * eg)
        vb_s[h] = v * beta
        qd_s[h] = q * eg
        qkg_s[h] = jnp.sum(q * k, -1, keepdims=True) * ones
        eg_s[h] = eg * ones
        kg_s[h] = k
        q = _rope128(hs_ref[:, C_RET + h * hd:C_RET + (h + 1) * hd], cos, sin)
        k = _rope128(hs_ref[:, C_RET + BRANCH_W + h * hd:C_RET + BRANCH_W + (h + 1) * hd], cos, sin) * SCALE
        qx_s[h] = q * rpar_ref[h, 0:1, 0:1]
        qkr_s[h] = jnp.sum(q * k, -1, keepdims=True) * ones
        kr_s[h] = k
        vr_s[h] = hs_ref[:, C_RET + 2 * BRANCH_W + h * hd:C_RET + 2 * BRANCH_W + (h + 1) * hd]
        mq_s[h] = hs_ref[:, C_MEM + h * hd:C_MEM + (h + 1) * hd]
        qb_ref[:, sl] = _rope128(hs_ref[:, C_DSA + h * hd:C_DSA + (h + 1) * hd], cos, sin)
        kb_ref[:, sl] = _rope128(hs_ref[:, C_DSA + BRANCH_W + h * hd:C_DSA + BRANCH_W + (h + 1) * hd], cos, sin)
    ki = _rope_idx(hs_ref[:, C_KI:C_KI + LANE], ci, sia, sib)
    ki_ref[...] = ki
    sself = jnp.zeros((nb, 1), F32)
    for h in range(IDX_HEADS):
        qi = _rope_idx(hs_ref[:, C_QI + h * LANE:C_QI + (h + 1) * LANE], ci, sia, sib)
        qi_ref[:, h * LANE:(h + 1) * LANE] = qi
        s = jnp.sum(qi * ki, -1, keepdims=True)
        sself = sself + jnp.maximum(s, 0.0) * (sm[:, 8 + h:9 + h] * IDX_SCALE)
    ss_ref[...] = sself * jnp.ones((1, LANE), F32)

    ii = lax.broadcasted_iota(jnp.int32, (hd, hd), 0)
    jj = lax.broadcasted_iota(jnp.int32, (hd, hd), 1)
    eye = jnp.where(ii == jj, 1.0, 0.0).astype(F32)

    def col_bcast(row):
        return _nt(eye, jnp.broadcast_to(row, (hd, hd)), HI)

    def per_batch(bb, carry):
        r = pl.ds(bb, 1)
        for h in range(N_HEADS):
            sl = hsl(h)
            s = sg_ref[bb, h]
            v_new = vb_s[h, r, :] - _mm(wk_s[h, r, :], s, HI)
            oa_s[h, r, :] = _mm(qd_s[h, r, :], s, HI) + qkg_s[h, r, :] * v_new
            sgo_ref[bb, h] = s * eg_s[h, r, :] + col_bcast(kg_s[h, r, :]) * v_new
            rs = sr_ref[bb, h]
            v = vr_s[h, r, :]
            oc_s[h, r, :] = _mm(qx_s[h, r, :], rs, HI) + qkr_s[h, r, :] * v
            sro_ref[bb, h] = rs * rpar_ref[h, 0:1, :] + col_bcast(kr_s[h, r, :]) * v
            lg = _nt(_b(mq_s[h, r, :]), _b(mk_ref[bb, :, sl])) * SCALE
            m = jnp.max(lg, -1, keepdims=True)
            p = jnp.exp(lg - m)
            oe_s[h, r, :] = _mm(_b(p), _b(mv_ref[bb, :, sl])) / jnp.sum(p, -1, keepdims=True)
        return carry

    lax.fori_loop(0, nb, per_batch, 0)

    for h in range(N_HEADS):
        sl = hsl(h)
        o = oa_s[h]
        o = o * lax.rsqrt(jnp.mean(o * o, -1, keepdims=True) + NORM_EPS) * gpar_ref[h, 2:3, :]
        ya_ref[:, sl] = o * _silu(hs_ref[:, C_GDNZ + h * hd:C_GDNZ + (h + 1) * hd])
        o = oc_s[h]
        d = o - jnp.mean(o, -1, keepdims=True)
        var = jnp.mean(d * d, -1, keepdims=True)
        yc_ref[:, sl] = d * lax.rsqrt(var + LN_EPS) * _silu(hs_ref[:, C_RET + 3 * BRANCH_W + h * hd:C_RET + 3 * BRANCH_W + (h + 1) * hd])
        ye_ref[:, sl] = oe_s[h] * _silu(hs_ref[:, C_MEM + BRANCH_W + h * hd:C_MEM + BRANCH_W + (h + 1) * hd])
    a, b = _lru_gates(conv[:, 3 * BRANCH_W:], wr_ref, wi_ref, br_ref[...], bi_ref[...], lam_ref[...])
    h_new = a * hl_ref[...] + b
    hlo_ref[...] = h_new
    yd_ref[...] = h_new * _silu(hs_ref[:, C_LRUZ:C_LRUZ + BRANCH_W])


def _decode(hs, hist_t, p, sg, sr, hl, mk3, mv3, tabs):
    nb = hs.shape[0]
    bw = jax.ShapeDtypeStruct((nb, BRANCH_W), F32)
    st = jax.ShapeDtypeStruct((nb, N_HEADS, HEAD_DIM, HEAD_DIM), F32)
    out_shape = [bw, bw, bw, bw, st, st, bw, bw, bw,
                 jax.ShapeDtypeStruct((nb, IDX_HEADS * LANE), F32), jax.ShapeDtypeStruct((nb, LANE), F32),
                 jax.ShapeDtypeStruct((nb, LANE), F32)]
    return pl.pallas_call(
        functools.partial(_dec_kernel, nb=nb),
        out_shape=out_shape,
        scratch_shapes=[pltpu.VMEM((N_HEADS, nb, HEAD_DIM), F32)] * 14,
        compiler_params=pltpu.CompilerParams(vmem_limit_bytes=VMEM_LIMIT),
        name="decode_step",
    )(hs, hist_t, p["conv_w"], p["conv_b"], p["gdn_par"], p["ret_par"], sg, sr, hl,
      p["lru_w_r"], p["lru_w_i"], p["lru_b_r"], p["lru_b_i"], p["lam"], mk3, mv3, *tabs)


PAGES_PER_STEP = 8


def _didx_kernel(pt_ref, qi_ref, w_ref, *rest):
    pages, o_ref = rest[:PAGES_PER_STEP], rest[PAGES_PER_STEP]
    q16 = _b(qi_ref[:, :IDX_DIM])
    w = w_ref[...]
    for j, pg in enumerate(pages):
        s = _nt(q16, _b(pg[...]))
        o_ref[j:j + 1, :] = jnp.sum(jnp.maximum(s, 0.0) * w, 0, keepdims=True)


def _didx(page_table, qi3, wcol, pool_ik, l):
    nb, n_pages = page_table.shape
    ng = n_pages // PAGES_PER_STEP
    page = lambda j: pl.BlockSpec((None, None, PAGE_SIZE, IDX_DIM),
                                  lambda b, g, pt, j=j: (l, pt[b, g * PAGES_PER_STEP + j], 0, 0))
    return pl.pallas_call(
        _didx_kernel,
        grid_spec=pltpu.PrefetchScalarGridSpec(
            num_scalar_prefetch=1, grid=(nb, ng),
            in_specs=[pl.BlockSpec((None, IDX_HEADS, LANE), lambda b, g, pt: (b, 0, 0)),
                      pl.BlockSpec((None, IDX_HEADS, 1), lambda b, g, pt: (b, 0, 0))]
            + [page(j) for j in range(PAGES_PER_STEP)],
            out_specs=pl.BlockSpec((None, None, PAGES_PER_STEP, PAGE_SIZE), lambda b, g, pt: (b, g, 0, 0))),
        out_shape=jax.ShapeDtypeStruct((nb, ng, PAGES_PER_STEP, PAGE_SIZE), F32),
        compiler_params=_cp(2),
        name="paged_indexer",
    )(page_table, qi3, wcol, *([pool_ik] * PAGES_PER_STEP))


NEG_SEL = -2e30
M_INIT = -1e30


def _dsel_kernel(sc_ref, ss_ref, tri_ref, bias_ref, sbias_ref, *, n_pages, ksel):
    key = sc_ref[...]
    kself = ss_ref[:, 0:1]

    def reduce2(x, op):
        return op(op(x, 1, keepdims=True), 0, keepdims=True)

    def count_ge(th):
        return reduce2(jnp.where(key >= th, 1.0, 0.0), jnp.sum) + jnp.where(kself >= th, 1.0, 0.0)

    def count_gt(th):
        return reduce2(jnp.where(key > th, 1.0, 0.0), jnp.sum) + jnp.where(kself > th, 1.0, 0.0)

    def next_above(th):
        return jnp.minimum(reduce2(jnp.where(key > th, key, jnp.inf), jnp.min), jnp.where(kself > th, kself, jnp.inf))

    lo = jnp.minimum(reduce2(key, jnp.min), kself)
    hi = jnp.maximum(reduce2(key, jnp.max), kself)
    thr, n_gt = _kth_threshold(count_ge, count_gt, next_above, lo, hi, float(ksel))
    eq = key == thr
    need = ksel - n_gt
    pre_in = _mm(jnp.where(eq, 1.0, 0.0).astype(BF16), tri_ref[...])
    tot = pre_in[:, PAGE_SIZE - 1:PAGE_SIZE]
    ii = lax.broadcasted_iota(jnp.int32, (n_pages, n_pages), 0)
    jj = lax.broadcasted_iota(jnp.int32, (n_pages, n_pages), 1)
    below = jnp.where(ii > jj, 1.0, 0.0).astype(BF16)
    pre = pre_in + _mm(below, _b(jnp.broadcast_to(tot, (n_pages, PAGE_SIZE))))
    bias_ref[...] = jnp.where(key > thr, 0.0, jnp.where(eq, jnp.where(pre <= need, 0.0, NEG_SEL), NEG_SEL))
    n_eq = pre[n_pages - 1:n_pages, PAGE_SIZE - 1:PAGE_SIZE]
    sb = jnp.where(kself > thr, 0.0, jnp.where(kself == thr, jnp.where(n_eq + 1.0 <= need, 0.0, NEG_SEL), NEG_SEL))
    sbias_ref[...] = sb * jnp.ones((1, LANE), F32)


def _dsel(scores3, sself3):
    nb, n_pages, _ = scores3.shape
    ksel = min(DSA_TOPK, (n_pages * PAGE_SIZE + 1) // 4)
    return pl.pallas_call(
        functools.partial(_dsel_kernel, n_pages=n_pages, ksel=ksel),
        grid=(nb,),
        in_specs=[pl.BlockSpec((None, n_pages, PAGE_SIZE), lambda b: (b, 0, 0)),
                  pl.BlockSpec((None, 1, LANE), lambda b: (b, 0, 0)),
                  pl.BlockSpec((LANE, LANE), lambda b: (0, 0))],
        out_specs=[pl.BlockSpec((None, n_pages, PAGE_SIZE), lambda b: (b, 0, 0)),
                   pl.BlockSpec((None, 1, LANE), lambda b: (b, 0, 0))],
        out_shape=[jax.ShapeDtypeStruct((nb, n_pages, PAGE_SIZE), F32), jax.ShapeDtypeStruct((nb, 1, LANE), F32)],
        compiler_params=_cp(1),
        name="sample_select",
    )(scores3, sself3, _tri128())


def _datt_kernel(pt_ref, q_ref, bias_ref, sbias_ref, kself_ref, vself_ref, z_ref, *rest, ng):
    kpages = rest[:PAGES_PER_STEP]
    vpages = rest[PAGES_PER_STEP:2 * PAGES_PER_STEP]
    o_ref, m_s, l_s, acc_s = rest[2 * PAGES_PER_STEP:]
    g = pl.program_id(1)

    @pl.when(g == 0)
    def _():
        m_s[...] = jnp.full(m_s.shape, M_INIT, F32)
        l_s[...] = jnp.zeros(l_s.shape, F32)
        acc_s[...] = jnp.zeros(acc_s.shape, F32)

    sub = lax.broadcasted_iota(jnp.int32, (8, BRANCH_W), 0)
    own = sub == lax.broadcasted_iota(jnp.int32, (8, BRANCH_W), 1) // HEAD_DIM
    qbd = jnp.where(own, jnp.broadcast_to(q_ref[...], (8, BRANCH_W)), 0.0)
    q16 = _b(qbd)

    def update(lg, pv):
        m_prev = m_s[...]
        m_new = jnp.maximum(m_prev, jnp.max(lg, -1, keepdims=True))
        alpha = jnp.exp(m_prev - m_new)
        p = jnp.exp(lg - m_new)
        l_s[...] = l_s[...] * alpha + jnp.sum(p, -1, keepdims=True)
        acc_s[...] = acc_s[...] * alpha + pv(p)
        m_s[...] = m_new

    for j in range(PAGES_PER_STEP):
        lg = _nt(q16, _b(kpages[j][...])) * SCALE + bias_ref[j:j + 1, :]
        update(lg, lambda p, j=j: _mm(_b(p), _b(vpages[j][...])))

    @pl.when(g == ng - 1)
    def _():
        lg = jnp.sum(qbd * kself_ref[...], -1, keepdims=True) * SCALE + sbias_ref[:, 0:1]
        update(lg, lambda p: p * vself_ref[...])
        o8 = acc_s[...] / l_s[...]
        o = jnp.sum(jnp.where(own, o8, 0.0), 0, keepdims=True)
        o_ref[...] = o * _silu(z_ref[...])


def _datt(page_table, qb3, bias3, sbias3, kself3, vself3, z3, pool_k, pool_v, l):
    nb, n_pages = page_table.shape
    ng = n_pages // PAGES_PER_STEP
    row = lambda w: pl.BlockSpec((None, 1, w), lambda b, g, pt: (b, 0, 0))
    page = lambda j: pl.BlockSpec((None, None, PAGE_SIZE, BRANCH_W),
                                  lambda b, g, pt, j=j: (l, pt[b, g * PAGES_PER_STEP + j], 0, 0))
    return pl.pallas_call(
        functools.partial(_datt_kernel, ng=ng),
        grid_spec=pltpu.PrefetchScalarGridSpec(
            num_scalar_prefetch=1, grid=(nb, ng),
            in_specs=[row(BRANCH_W), pl.BlockSpec((None, PAGES_PER_STEP, PAGE_SIZE), lambda b, g, pt: (b, g, 0)),
                      row(LANE), row(BRANCH_W), row(BRANCH_W), row(BRANCH_W)]
            + [page(j) for j in range(PAGES_PER_STEP)] * 2,
            out_specs=row(BRANCH_W),
            scratch_shapes=[pltpu.VMEM((8, 1), F32), pltpu.VMEM((8, 1), F32), pltpu.VMEM((8, BRANCH_W), F32)]),
        out_shape=jax.ShapeDtypeStruct((nb, 1, BRANCH_W), F32),
        compiler_params=_cp(2),
        name="paged_attention",
    )(page_table, qb3, bias3, sbias3, kself3, vself3, z3, *([pool_k] * PAGES_PER_STEP), *([pool_v] * PAGES_PER_STEP))


def _sample_layer(x, x16, p, tabs, conv_buf, sg, sr, hl, mem_k, mem_v, pool_k, pool_v, pool_ik, page_table, l):
    nb = x.shape[0]
    hs = _matmul(x16, p["w_main"], F32, nb, 512)
    gs = _matmul(x16, p["w_gate"], BF16, nb, 1024)
    mk3 = mem_k.reshape(nb, -1, BRANCH_W)
    mv3 = mem_v.reshape(nb, -1, BRANCH_W)
    (y_a, y_c, y_d, y_e, sg_new, sr_new, hl_new, qb, kb, qi, ki, sself) = _decode(
        hs, conv_buf.transpose(1, 0, 2), p, sg, sr, hl, mk3, mv3, tabs)
    wcol = (hs[:, C_SMALL + 8:C_SMALL + 16] * IDX_SCALE)[:, :, None]
    scores = _didx(page_table, qi.reshape(nb, IDX_HEADS, LANE), wcol, pool_ik, l)
    n_pages = page_table.shape[1]
    bias3, sbias3 = _dsel(scores.reshape(nb, n_pages, PAGE_SIZE), sself[:, None, :])
    v_self = hs[:, C_DSA + 1024:C_DSA + 1536]
    y_b = _datt(page_table, qb[:, None, :], bias3, sbias3, kb[:, None, :], v_self[:, None, :],
                hs[:, None, C_DSA + 1536:C_DSA + 2048],
                pool_k.reshape(*pool_k.shape[:2], PAGE_SIZE, BRANCH_W), pool_v.reshape(*pool_v.shape[:2], PAGE_SIZE, BRANCH_W), l)
    merged = _merge([y_a, y_b.reshape(nb, BRANCH_W), y_c, y_d, y_e], gs, p["w_branch"], nb)
    y, y16 = _outnorm(merged, x.reshape(nb, D_MODEL), p["w_out"], p["ln_g"], p["ln_b"], nb)
    new_conv = jnp.concatenate([conv_buf[:, 1:], hs[:, None, C_CONV:C_CONV + CONV_DIM]], 1)
    return (y.reshape(nb, 1, D_MODEL), y16, (new_conv, sg_new, sr_new, hl_new),
            (kb.reshape(nb, 1, N_HEADS, HEAD_DIM), v_self.reshape(nb, 1, N_HEADS, HEAD_DIM), ki[:, None, :IDX_DIM]))


def kernel(x_prompt, x_sample, cache_conv, cache_dsa_k, cache_dsa_v, cache_dsa_idx_k, cache_mem_k, cache_mem_v,
           state_gdn, state_ret, state_lru, page_table, mem_prompt, w_in, conv_w, conv_b, gdn_a_log, gdn_dt_bias,
           gdn_norm_w, lru_lambda, lru_w_r, lru_b_r, lru_w_i, lru_b_i, w_mem_kv, w_branch, w_out, ln_g, ln_b):
    bsz, t, _ = x_prompt.shape
    nb = x_sample.shape[0]
    depth = w_in.shape[0]
    past = page_table.shape[1] * PAGE_SIZE
    tabs_p = _rope_tables(np.arange(t))
    tabs_s = _rope_tables(np.asarray([past]))
    mem16 = _b(mem_prompt.reshape(-1, D_MODEL))
    xp, xs = x_prompt, x_sample
    xp16, xs16 = _b(xp.reshape(-1, D_MODEL)), _b(xs.reshape(-1, D_MODEL))
    acc_p, acc_s = [], []
    for l in range(depth):
        p = _layer_params(l, w_in, conv_w, conv_b, gdn_a_log, gdn_dt_bias, gdn_norm_w, lru_lambda, lru_w_r, lru_b_r,
                          lru_w_i, lru_b_i, w_mem_kv, w_branch, w_out, ln_g, ln_b)
        xp, xp16, st, rows, mk, mv = _prompt_layer(xp, xp16, mem16, p, tabs_p)
        acc_p.append(st + rows + (mk, mv))
        xs, xs16, st, rows = _sample_layer(xs, xs16, p, tabs_s, cache_conv[l], state_gdn[l], state_ret[l], state_lru[l],
                                           cache_mem_k[l], cache_mem_v[l], cache_dsa_k, cache_dsa_v,
                                           cache_dsa_idx_k, page_table, l)
        acc_s.append(st + rows)
    stack = lambda acc, i: jnp.stack([a[i] for a in acc])
    return (xp, xs) + tuple(stack(acc_p, i) for i in range(9)) + tuple(stack(acc_s, i) for i in range(7))
```

```python
import functools
import math

import numpy as np
import jax
import jax.numpy as jnp
from jax import lax
from jax.experimental import pallas as pl
from jax.experimental.pallas import tpu as pltpu

F32 = jnp.float32
BF16 = jnp.bfloat16
HI = lax.Precision.HIGHEST

D_MODEL = 2048
HEAD_DIM = 128
BRANCH_W = 512
N_HEADS = 4
N_BRANCH = 5
CONV_W = 4
CONV_DIM = 2048
GDN_CHUNK = 64
GDN_GROUP = 256
RET_CHUNK = 256
DSA_TOPK = 256
IDX_HEADS = 8
IDX_DIM = 64
LRU_C = 8.0
ROPE_THETA = 10000.0
PAGE_SIZE = 128
LN_EPS = 1e-5
NORM_EPS = 1e-6
DEPTH = 2
DEEPNORM_ALPHA = (2 * DEPTH) ** 0.25
IN_SPLITS = (CONV_DIM, BRANCH_W, N_HEADS, N_HEADS,
             BRANCH_W, BRANCH_W, BRANCH_W, BRANCH_W, IDX_HEADS * IDX_DIM, IDX_DIM, IDX_HEADS,
             BRANCH_W, BRANCH_W, BRANCH_W, BRANCH_W,
             BRANCH_W,
             BRANCH_W, BRANCH_W,
             N_BRANCH * D_MODEL)

LANE = 128
C_CONV = 0
C_GDNZ = 2048
C_DSA = 2560
C_RET = 4608
C_LRUZ = 6656
C_MEM = 7168
C_QI = 8192
C_KI = 9216
C_SMALL = 9344
N_MAIN = 9728
N_GATE = N_BRANCH * D_MODEL
SCALE = HEAD_DIM ** -0.5
IDX_SCALE = (IDX_HEADS * IDX_DIM) ** -0.5
NEG = -1e30
VMEM_LIMIT = 56 * 1024 * 1024


def _cp(n_grid):
    return pltpu.CompilerParams(dimension_semantics=("arbitrary",) * n_grid, vmem_limit_bytes=VMEM_LIMIT)


def _b(x):
    return x.astype(BF16)


def _mm(a, b, precision=None):
    return jnp.dot(a, b, precision=precision, preferred_element_type=F32)


def _nt(a, b, precision=None):
    return lax.dot_general(a, b, (((1,), (1,)), ((), ())), precision=precision, preferred_element_type=F32)


def _tn(a, b, precision=None):
    return lax.dot_general(a, b, (((0,), (0,)), ((), ())), precision=precision, preferred_element_type=F32)


def _split3(x):
    hi = _b(x)
    r = x - hi.astype(F32)
    mid = _b(r)
    return hi, mid, _b(r - mid.astype(F32))


def _dot_pieces(dot, sel16, pieces):
    hi, mid, lo = pieces
    return (dot(sel16, lo) + dot(sel16, mid)) + dot(sel16, hi)


def _silu(x):
    return x * jax.nn.sigmoid(x)


def _rope128(x, cos, sin):
    return x * cos + pltpu.roll(x, 64, 1) * sin


def _rope_idx(x, cos, sina, sinb):
    return x * cos + pltpu.roll(x, 96, 1) * sina + pltpu.roll(x, 32, 1) * sinb


def _rope_tables(pos):
    pos = np.asarray(pos, np.float64)[:, None]
    ang = pos * (ROPE_THETA ** (-np.arange(64, dtype=np.float64) / 64))[None, :]
    c, s = np.cos(ang), np.sin(ang)
    cos128 = np.concatenate([c, c], 1)
    sin128 = np.concatenate([-s, s], 1)
    ang32 = pos * (ROPE_THETA ** (-np.arange(32, dtype=np.float64) / 32))[None, :]
    c, s = np.cos(ang32), np.sin(ang32)
    z = np.zeros_like(c)
    cos_i = np.concatenate([c, c, z, z], 1)
    sin_ia = np.concatenate([-s, z, z, z], 1)
    sin_ib = np.concatenate([z, s, z, z], 1)
    return tuple(jnp.asarray(a, F32) for a in (cos128, sin128, cos_i, sin_ia, sin_ib))


def _mm_kernel(x_ref, w_ref, o_ref):
    o_ref[...] = _mm(x_ref[...], w_ref[...]).astype(o_ref.dtype)


def _matmul(x, w, out_dtype, tm, tn):
    m, k = x.shape
    n = w.shape[1]
    return pl.pallas_call(
        _mm_kernel,
        grid=(m // tm, n // tn),
        in_specs=[pl.BlockSpec((tm, k), lambda i, j: (i, 0)),
                  pl.BlockSpec((k, tn), lambda i, j: (0, j))],
        out_specs=pl.BlockSpec((tm, tn), lambda i, j: (i, j)),
        out_shape=jax.ShapeDtypeStruct((m, n), out_dtype),
        compiler_params=_cp(2),
        name="proj_matmul",
    )(x, w)


def _mm_nt_kernel(x_ref, w_ref, o_ref):
    o_ref[...] = _nt(x_ref[...], w_ref[...]).astype(o_ref.dtype)


def _matmul_nt(x, wt, out_dtype, tm, tn):
    m, k = x.shape
    n = wt.shape[0]
    return pl.pallas_call(
        _mm_nt_kernel,
        grid=(m // tm, n // tn),
        in_specs=[pl.BlockSpec((tm, k), lambda i, j: (i, 0)),
                  pl.BlockSpec((tn, k), lambda i, j: (j, 0))],
        out_specs=pl.BlockSpec((tm, tn), lambda i, j: (i, j)),
        out_shape=jax.ShapeDtypeStruct((m, n), out_dtype),
        compiler_params=_cp(2),
        name="proj_matmul_nt",
    )(x, wt)


W_PREP_COLS = 128


def _relayout_kernel(w_ref, main_ref, gate_ref):
    offs = [int(o) for o in np.concatenate([[0], np.cumsum(IN_SPLITS)])]
    cols = main_ref.shape[1]

    def put(dst, src, n):
        main_ref[dst:dst + n, :] = _b(w_ref[src:src + n, :])

    def put_padded(dst, pieces):
        parts = [w_ref[src:src + n, :] for src, n in pieces]
        parts.append(jnp.zeros((LANE - sum(n for _, n in pieces), cols), F32))
        main_ref[dst:dst + LANE, :] = _b(jnp.concatenate(parts, 0))

    put(C_CONV, offs[0], CONV_DIM)
    put(C_GDNZ, offs[1], BRANCH_W)
    put(C_DSA, offs[4], 4 * BRANCH_W)
    put(C_RET, offs[11], 4 * BRANCH_W)
    put(C_LRUZ, offs[15], BRANCH_W)
    put(C_MEM, offs[16], 2 * BRANCH_W)
    for h in range(IDX_HEADS):
        put_padded(C_QI + h * LANE, [(offs[8] + h * IDX_DIM, IDX_DIM)])
    put_padded(C_KI, [(offs[9], IDX_DIM)])
    put_padded(C_SMALL, [(offs[2], 2 * N_HEADS), (offs[10], IDX_HEADS)])
    main_ref[C_SMALL + LANE:N_MAIN, :] = jnp.zeros((N_MAIN - C_SMALL - LANE, cols), BF16)
    gate_ref[...] = _b(w_ref[offs[18]:offs[18] + N_GATE, :])


def _prep_w_in(w_in, l):
    wt = jnp.swapaxes(w_in, 1, 2)
    _, n_in, d = wt.shape
    return pl.pallas_call(
        _relayout_kernel,
        grid=(d // W_PREP_COLS,),
        in_specs=[pl.BlockSpec((None, n_in, W_PREP_COLS), lambda i: (l, 0, i))],
        out_specs=[pl.BlockSpec((N_MAIN, W_PREP_COLS), lambda i: (0, i)),
                   pl.BlockSpec((N_GATE, W_PREP_COLS), lambda i: (0, i))],
        out_shape=[jax.ShapeDtypeStruct((N_MAIN, d), BF16), jax.ShapeDtypeStruct((N_GATE, d), BF16)],
        compiler_params=_cp(1),
        name="w_in_relayout",
    )(wt)


def _ropek_kernel(k_ref, v_ref, ki_ref, cos_ref, sin_ref, ci_ref, sia_ref, sib_ref,
                  kb_ref, k16_ref, v16_ref, kif_ref, ki16_ref):
    cos, sin = cos_ref[...], sin_ref[...]
    for h in range(N_HEADS):
        sl = slice(h * HEAD_DIM, (h + 1) * HEAD_DIM)
        r = _rope128(k_ref[:, sl], cos, sin)
        kb_ref[:, sl] = r
        k16_ref[:, sl] = _b(r)
    v16_ref[...] = _b(v_ref[...])
    ki = _rope_idx(ki_ref[...], ci_ref[...], sia_ref[...], sib_ref[...])
    kif_ref[...] = ki
    ki16_ref[...] = _b(ki)


def _rope_k(h3, tabs, tr):
    bsz, t, _ = h3.shape
    cb = lambda c, w: c // w
    tab = pl.BlockSpec((tr, LANE), lambda b, i: (i, 0))
    row = lambda w, c: pl.BlockSpec((None, tr, w), lambda b, i: (b, i, c))
    return pl.pallas_call(
        _ropek_kernel,
        grid=(bsz, t // tr),
        in_specs=[row(512, cb(C_DSA + 512, 512)), row(512, cb(C_DSA + 1024, 512)), row(LANE, cb(C_KI, LANE)),
                  tab, tab, tab, tab, tab],
        out_specs=[row(512, 0), row(512, 0), row(512, 0), row(LANE, 0), row(LANE, 0)],
        out_shape=[jax.ShapeDtypeStruct((bsz, t, 512), F32), jax.ShapeDtypeStruct((bsz, t, 512), BF16),
                   jax.ShapeDtypeStruct((bsz, t, 512), BF16), jax.ShapeDtypeStruct((bsz, t, LANE), F32),
                   jax.ShapeDtypeStruct((bsz, t, LANE), BF16)],
        compiler_params=_cp(2),
        name="rope_k",
    )(h3, h3, h3, *tabs)


def _gdn_kernel(q_ref, k_ref, v_ref, z_ref, sm_ref, wq_ref, wk_ref, wv_ref, bq_ref, bk_ref, bv_ref,
                hq_ref, hk_ref, hv_ref, par_ref, s0_ref, y_ref, sfin_ref,
                scq, sck, scv, o0_s, qe_s, n_s, p_s, gl_s, *, t, c, gr):
    h = pl.program_id(1)
    nc = t // c
    cpg = gr // c
    for src, hist, scr in ((q_ref, hq_ref, scq), (k_ref, hk_ref, sck), (v_ref, hv_ref, scv)):
        scr[pl.ds(0, 8), :] = hist[...]
        scr[pl.ds(8, t), :] = src[...]
    ii = lax.broadcasted_iota(jnp.int32, (gr, gr), 0)
    jj = lax.broadcasted_iota(jnp.int32, (gr, gr), 1)
    same = (ii // c) == (jj // c)
    tril = jnp.logical_and(same, ii >= jj)
    strict = jnp.logical_and(same, ii > jj)
    tril16 = jnp.where(tril, 1.0, 0.0).astype(BF16)
    same16 = jnp.where(same, 1.0, 0.0).astype(BF16)
    lane = lax.broadcasted_iota(jnp.int32, (gr, LANE), 1)
    e0 = jnp.where(lane == 0, 1.0, 0.0).astype(BF16)
    neg_a = -jnp.exp(par_ref[0:1, 0:1])
    dt_bias = par_ref[1:2, 0:1]
    norm_w = par_ref[2:3, :]

    def conv(scr, w_ref, b_ref, r0):
        acc = b_ref[...] + scr[pl.ds(r0 + 5, gr), :] * w_ref[0:1, :]
        for j in range(1, CONV_W):
            acc = acc + scr[pl.ds(r0 + 5 + j, gr), :] * w_ref[j:j + 1, :]
        return acc

    def prep(gi, carry):
        r0 = pl.multiple_of(gi * gr, gr)
        qc = _silu(conv(scq, wq_ref, bq_ref, r0))
        kc = _silu(conv(sck, wk_ref, bk_ref, r0))
        vc = _silu(conv(scv, wv_ref, bv_ref, r0))
        qn = qc * lax.rsqrt(jnp.sum(qc * qc, -1, keepdims=True) + NORM_EPS) * SCALE
        kn = kc * lax.rsqrt(jnp.sum(kc * kc, -1, keepdims=True) + NORM_EPS)
        sm = sm_ref[pl.ds(r0, gr), :]
        a_col = jnp.sum(jnp.where(lane == h, sm, 0.0), -1, keepdims=True)
        b_col = jnp.sum(jnp.where(lane == N_HEADS + h, sm, 0.0), -1, keepdims=True)
        beta = jax.nn.sigmoid(b_col)
        gb = _split3(jnp.broadcast_to(neg_a * jax.nn.softplus(a_col + dt_bias), (gr, LANE)))
        gcb = _dot_pieces(_mm, tril16, gb)
        glb = _dot_pieces(_mm, same16, gb)
        gc = gcb[:, 0:1]
        gcr = _dot_pieces(_nt, e0, _split3(gcb))
        decay = jnp.where(tril, jnp.exp(jnp.minimum(gc - gcr, 0.0)), 0.0)
        q16, k16 = _b(qn), _b(kn)
        a = jnp.where(strict, beta * _nt(k16, k16) * decay, 0.0)
        x = -a
        r = x
        for _ in range(int(math.log2(c)) - 1):
            x16 = _b(x)
            x = _mm(x16, x16)
            r = r + x + _mm(_b(r), _b(x))
        egc = jnp.exp(gc)
        r16 = _b(r)
        rhs_v = vc * beta
        rhs_k = kn * (beta * egc)
        u = rhs_v + _mm(r16, _b(rhs_v))
        w = rhs_k + _mm(r16, _b(rhs_k))
        qk = _nt(q16, k16) * decay
        qd = qn * egc
        kt = kn * jnp.exp(glb[:, 0:1] - gc)
        egl = jnp.exp(glb)
        qk16 = _b(qk)
        qe = qd - _mm(qk16, _b(w))
        o0 = _mm(qk16, _b(u))
        for j in range(cpg):
            ci = gi * cpg + j
            rows = slice(j * c, (j + 1) * c)
            o0_s[ci] = o0[rows]
            qe_s[ci] = _b(qe[rows])
            n_s[ci] = _tn(kt[rows], u[rows])
            p_s[ci] = _b(_tn(kt[rows], w[rows]))
            gl_s[ci] = egl[j * c:j * c + 1]
        return carry

    lax.fori_loop(0, t // gr, prep, 0, unroll=math.gcd(t // gr, 2))

    def step(ci, s):
        r0 = pl.multiple_of(ci * c, c)
        s16 = _b(s)
        o = o0_s[ci] + _mm(qe_s[ci], s16)
        s = s * gl_s[ci] + (n_s[ci] - _mm(p_s[ci], s16))
        o = o * lax.rsqrt(jnp.mean(o * o, -1, keepdims=True) + NORM_EPS) * norm_w
        y_ref[pl.ds(r0, c), :] = o * _silu(z_ref[pl.ds(r0, c), :])
        return s

    sfin_ref[...] = lax.fori_loop(0, nc, step, s0_ref[...])


def _gdn(h3, hist8, conv_w, conv_b, par, s0):
    bsz, t, _ = h3.shape
    c = math.gcd(t, GDN_CHUNK)
    gr = math.gcd(t, GDN_GROUP)
    nc = t // c
    col = lambda blk: pl.BlockSpec((None, t, LANE), lambda b, h, blk=blk: (b, 0, blk + h))
    wsp = lambda blk: pl.BlockSpec((CONV_W, LANE), lambda b, h, blk=blk: (0, blk + h))
    bsp = lambda blk: pl.BlockSpec((1, LANE), lambda b, h, blk=blk: (0, blk + h))
    hsp = lambda blk: pl.BlockSpec((None, 8, LANE), lambda b, h, blk=blk: (b, 0, blk + h))
    st = pl.BlockSpec((None, None, HEAD_DIM, HEAD_DIM), lambda b, h: (b, h, 0, 0))
    return pl.pallas_call(
        functools.partial(_gdn_kernel, t=t, c=c, gr=gr),
        grid=(bsz, N_HEADS),
        in_specs=[col(0), col(4), col(8), col(C_GDNZ // LANE),
                  pl.BlockSpec((None, t, LANE), lambda b, h: (b, 0, C_SMALL // LANE)),
                  wsp(0), wsp(4), wsp(8), bsp(0), bsp(4), bsp(8), hsp(0), hsp(4), hsp(8),
                  pl.BlockSpec((None, 8, LANE), lambda b, h: (h, 0, 0)), st],
        out_specs=[pl.BlockSpec((None, t, LANE), lambda b, h: (b, 0, h)), st],
        out_shape=[jax.ShapeDtypeStruct((bsz, t, BRANCH_W), F32),
                   jax.ShapeDtypeStruct((bsz, N_HEADS, HEAD_DIM, HEAD_DIM), F32)],
        scratch_shapes=[pltpu.VMEM((t + 8, LANE), F32)] * 3 + [
            pltpu.VMEM((nc, c, LANE), F32), pltpu.VMEM((nc, c, LANE), BF16),
            pltpu.VMEM((nc, HEAD_DIM, HEAD_DIM), F32), pltpu.VMEM((nc, HEAD_DIM, HEAD_DIM), BF16),
            pltpu.VMEM((nc, 1, LANE), F32)],
        compiler_params=_cp(2),
        name="gdn",
    )(h3, h3, h3, h3, h3, conv_w, conv_w, conv_w, conv_b, conv_b, conv_b, hist8, hist8, hist8, par, s0)


def _ret_kernel(q_ref, k_ref, v_ref, z_ref, cos_ref, sin_ref, par_ref, r0_ref, y_ref, rfin_ref, *, t, c):
    nc = t // c
    lg = jnp.log(par_ref[0:1, 0:1])
    ii = lax.broadcasted_iota(jnp.int32, (c, c), 0)
    jj = lax.broadcasted_iota(jnp.int32, (c, c), 1)
    dmat = jnp.where(ii >= jj, jnp.exp((ii - jj).astype(F32) * lg), 0.0)
    icol = lax.broadcasted_iota(jnp.int32, (c, 1), 0).astype(F32)
    xi = jnp.exp((icol + 1.0) * lg)
    zeta = jnp.exp((c - 1.0 - icol) * lg)
    g_c = jnp.exp(c * lg)

    def step(ci, r):
        rows = pl.ds(pl.multiple_of(ci * c, c), c)
        cos, sin = cos_ref[rows, :], sin_ref[rows, :]
        q = _rope128(q_ref[rows, :], cos, sin)
        k = _rope128(k_ref[rows, :], cos, sin) * SCALE
        v = v_ref[rows, :]
        q16, k16, v16 = _b(q), _b(k), _b(v)
        o = _mm(_b(_nt(q16, k16) * dmat), v16) + _mm(_b(q * xi), _b(r))
        r = r * g_c + _tn(k * zeta, v)
        mu = jnp.mean(o, -1, keepdims=True)
        d = o - mu
        var = jnp.mean(d * d, -1, keepdims=True)
        y_ref[rows, :] = d * lax.rsqrt(var + LN_EPS) * _silu(z_ref[rows, :])
        return r

    rfin_ref[...] = lax.fori_loop(0, nc, step, r0_ref[...], unroll=math.gcd(nc, CHUNK_UNROLL))


def _ret_par():
    gamma = (1.0 - 2.0 ** (-5.0 - np.arange(N_HEADS, dtype=np.float64))).astype(np.float32)
    return jnp.asarray(np.broadcast_to(gamma[:, None, None], (N_HEADS, 8, LANE)).copy(), F32)


def _ret(h3, cos, sin, par, r0):
    bsz, t, _ = h3.shape
    c = math.gcd(t, RET_CHUNK)
    col = lambda blk: pl.BlockSpec((None, t, LANE), lambda b, h, blk=blk: (b, 0, blk + h))
    tab = pl.BlockSpec((t, LANE), lambda b, h: (0, 0))
    st = pl.BlockSpec((None, None, HEAD_DIM, HEAD_DIM), lambda b, h: (b, h, 0, 0))
    base = C_RET // LANE
    return pl.pallas_call(
        functools.partial(_ret_kernel, t=t, c=c),
        grid=(bsz, N_HEADS),
        in_specs=[col(base), col(base + 4), col(base + 8), col(base + 12), tab, tab,
                  pl.BlockSpec((None, 8, LANE), lambda b, h: (h, 0, 0)), st],
        out_specs=[pl.BlockSpec((None, t, LANE), lambda b, h: (b, 0, h)), st],
        out_shape=[jax.ShapeDtypeStruct((bsz, t, BRANCH_W), F32),
                   jax.ShapeDtypeStruct((bsz, N_HEADS, HEAD_DIM, HEAD_DIM), F32)],
        compiler_params=_cp(2),
        name="retention",
    )(h3, h3, h3, h3, cos, sin, par, r0)


def _lru_gates(x, wr_ref, wi_ref, br, bi, lam):
    rs, is_ = [], []
    for h in range(N_HEADS):
        sl = slice(h * HEAD_DIM, (h + 1) * HEAD_DIM)
        xh = _b(x[:, sl])
        rs.append(_mm(xh, wr_ref[h]))
        is_.append(_mm(xh, wi_ref[h]))
    r = jax.nn.sigmoid(jnp.concatenate(rs, 1) + br)
    ig = jax.nn.sigmoid(jnp.concatenate(is_, 1) + bi)
    log_a = -LRU_C * r * jax.nn.softplus(-lam)
    a = jnp.exp(log_a)
    th = jnp.tanh(log_a)
    one_m_a2 = -2.0 * th / (1.0 - th)
    return a, jnp.sqrt(one_m_a2) * (ig * x)


def _lru_kernel(x_ref, z_ref, w_ref, b_ref, hist_ref, wr_ref, wi_ref, br_ref, bi_ref, lam_ref, h0_ref,
                y_ref, hfin_ref, scx, a_s, b_s, *, t, rb):
    for h in range(N_HEADS):
        sl = slice(h * HEAD_DIM, (h + 1) * HEAD_DIM)
        scx[h, pl.ds(0, 8), :] = hist_ref[:, sl]
        scx[h, pl.ds(8, t), :] = x_ref[:, sl]
    br, bi, lam = br_ref[...], bi_ref[...], lam_ref[...]

    def gates(i, carry):
        r0 = pl.multiple_of(i * rb, rb)
        xs = []
        for h in range(N_HEADS):
            sl = slice(h * HEAD_DIM, (h + 1) * HEAD_DIM)
            xh = b_ref[:, sl] + scx[h, pl.ds(r0 + 5, rb), :] * w_ref[0:1, sl]
            for j in range(1, CONV_W):
                xh = xh + scx[h, pl.ds(r0 + 5 + j, rb), :] * w_ref[j:j + 1, sl]
            xs.append(xh)
        x = jnp.concatenate(xs, 1)
        a, b = _lru_gates(x, wr_ref, wi_ref, br, bi, lam)
        a_s[pl.ds(r0, rb), :] = a
        b_s[pl.ds(r0, rb), :] = b
        return carry

    lax.fori_loop(0, t // rb, gates, 0)

    def scan(i, h):
        h = a_s[pl.ds(i, 1), :] * h + b_s[pl.ds(i, 1), :]
        b_s[pl.ds(i, 1), :] = h
        return h

    hfin_ref[...] = lax.fori_loop(0, t, scan, h0_ref[...], unroll=8)

    def gate_out(i, carry):
        rows = pl.ds(pl.multiple_of(i * rb, rb), rb)
        y_ref[rows, :] = b_s[rows, :] * _silu(z_ref[rows, :])
        return carry

    lax.fori_loop(0, t // rb, gate_out, 0)


def _lru(h3, hist8, conv_w, conv_b, w_r, w_i, b_r, b_i, lam, h0):
    bsz, t, _ = h3.shape
    rb = math.gcd(t, 256)
    xblk = (C_CONV + 3 * BRANCH_W) // BRANCH_W
    vec = pl.BlockSpec((1, BRANCH_W), lambda b: (0, 0))
    wsp = pl.BlockSpec((N_HEADS, HEAD_DIM, HEAD_DIM), lambda b: (0, 0, 0))
    return pl.pallas_call(
        functools.partial(_lru_kernel, t=t, rb=rb),
        grid=(bsz,),
        in_specs=[pl.BlockSpec((None, t, BRANCH_W), lambda b: (b, 0, xblk)),
                  pl.BlockSpec((None, t, BRANCH_W), lambda b: (b, 0, C_LRUZ // BRANCH_W)),
                  pl.BlockSpec((CONV_W, BRANCH_W), lambda b: (0, xblk)),
                  pl.BlockSpec((1, BRANCH_W), lambda b: (0, xblk)),
                  pl.BlockSpec((None, 8, BRANCH_W), lambda b: (b, 0, xblk)),
                  wsp, wsp, vec, vec, vec,
                  pl.BlockSpec((None, 1, BRANCH_W), lambda b: (b, 0, 0))],
        out_specs=[pl.BlockSpec((None, t, BRANCH_W), lambda b: (b, 0, 0)),
                   pl.BlockSpec((None, 1, BRANCH_W), lambda b: (b, 0, 0))],
        out_shape=[jax.ShapeDtypeStruct((bsz, t, BRANCH_W), F32), jax.ShapeDtypeStruct((bsz, 1, BRANCH_W), F32)],
        scratch_shapes=[pltpu.VMEM((N_HEADS, t + 8, HEAD_DIM), F32), pltpu.VMEM((t, BRANCH_W), F32),
                        pltpu.VMEM((t, BRANCH_W), F32)],
        compiler_params=_cp(1),
        name="rglru",
    )(h3, h3, conv_w, conv_b, hist8, w_r, w_i, b_r, b_i, lam, h0)


CHUNK_UNROLL = 4
BISECT_STEPS = 18


def _kth_threshold(count_ge, count_gt, next_above, lo, hi, k):
    c_hi = count_ge(hi)
    top = c_hi >= k
    lo = jnp.where(top, hi, lo)

    def halve(_, c):
        lo, hi = c
        mid = lo + (hi - lo) * 0.5
        up = count_ge(mid) >= k
        return jnp.where(up, mid, lo), jnp.where(up, hi, mid)

    lo, _ = lax.fori_loop(0, BISECT_STEPS, halve, (lo, hi))

    def more(c):
        return jnp.max(c[1] - k) > 0.5

    def walk(c):
        thr, c_gt = c
        nxt = jnp.where(c_gt > k, next_above(thr), thr)
        return nxt, count_gt(nxt)

    return lax.while_loop(more, walk, (lo, count_gt(lo)))


def _dsa_kernel(q_ref, z_ref, qi_ref, sm_ref, cos_ref, sin_ref, ci_ref, sia_ref, sib_ref,
                k_ref, v_ref, ki_ref, tri_ref, o_ref, sc_s, bias_s, *, tq, t, ksel):
    i = pl.program_id(1)
    ki16 = ki_ref[...]
    sm = sm_ref[...]
    ci, sia, sib = ci_ref[...], sia_ref[...], sib_ref[...]
    score = jnp.zeros((tq, t), F32)
    for h in range(IDX_HEADS):
        x = _rope_idx(qi_ref[:, h * LANE:(h + 1) * LANE], ci, sia, sib)
        s = _nt(_b(x), ki16)
        w = sm[:, 8 + h:9 + h] * IDX_SCALE
        score = score + jnp.maximum(s, 0.0) * w
    qpos = i * tq + lax.broadcasted_iota(jnp.int32, (tq, 1), 0)
    kpos = lax.broadcasted_iota(jnp.int32, (1, t), 1)
    valid = kpos <= qpos
    sc_s[...] = jnp.where(valid, score, -jnp.inf)
    k_row = jnp.minimum(float(ksel), (qpos + 1).astype(F32))

    def count_ge(th):
        return jnp.sum(jnp.where(sc_s[...] >= th, 1.0, 0.0), -1, keepdims=True)

    def count_gt(th):
        return jnp.sum(jnp.where(sc_s[...] > th, 1.0, 0.0), -1, keepdims=True)

    def next_above(th):
        s = sc_s[...]
        return jnp.min(jnp.where(s > th, s, jnp.inf), -1, keepdims=True)

    lo = jnp.min(jnp.where(valid, score, jnp.inf), -1, keepdims=True)
    hi = jnp.max(sc_s[...], -1, keepdims=True)
    thr, n_gt = _kth_threshold(count_ge, count_gt, next_above, lo, hi, k_row)
    need = k_row - n_gt
    run = jnp.zeros((tq, 1), F32)
    tri = tri_ref[...]
    for cb in range(t // LANE):
        sl = slice(cb * LANE, (cb + 1) * LANE)
        sc = sc_s[:, sl]
        eq = sc == thr
        pre = _mm(jnp.where(eq, 1.0, 0.0).astype(BF16), tri) + run
        bias_s[:, sl] = jnp.where(sc > thr, 0.0, jnp.where(eq, jnp.where(pre <= need, 0.0, NEG), NEG))
        run = pre[:, LANE - 1:LANE]
    bias = bias_s[...]
    cos, sin = cos_ref[...], sin_ref[...]
    for h in range(N_HEADS):
        sl = slice(h * HEAD_DIM, (h + 1) * HEAD_DIM)
        q = _rope128(q_ref[:, sl], cos, sin)
        lg = _nt(_b(q), k_ref[:, sl]) * SCALE + bias
        m = jnp.max(lg, -1, keepdims=True)
        p = jnp.exp(lg - m)
        l = jnp.sum(p, -1, keepdims=True)
        o = _mm(_b(p), v_ref[:, sl]) / l
        o_ref[:, sl] = o * _silu(z_ref[:, sl])


def _tri128():
    return jnp.asarray(np.triu(np.ones((LANE, LANE), np.float32)), BF16)


def _dsa(h3, k16, v16, ki16, tabs, tq):
    bsz, t, _ = h3.shape
    ksel = min(DSA_TOPK, t // 4)
    cos, sin, ci, sia, sib = tabs
    qtab = pl.BlockSpec((tq, LANE), lambda b, i: (i, 0))
    full = lambda w: pl.BlockSpec((None, t, w), lambda b, i: (b, 0, 0))
    return pl.pallas_call(
        functools.partial(_dsa_kernel, tq=tq, t=t, ksel=ksel),
        grid=(bsz, t // tq),
        in_specs=[pl.BlockSpec((None, tq, 512), lambda b, i: (b, i, C_DSA // 512)),
                  pl.BlockSpec((None, tq, 512), lambda b, i: (b, i, (C_DSA + 1536) // 512)),
                  pl.BlockSpec((None, tq, 1024), lambda b, i: (b, i, C_QI // 1024)),
                  pl.BlockSpec((None, tq, LANE), lambda b, i: (b, i, C_SMALL // LANE)),
                  qtab, qtab, qtab, qtab, qtab,
                  full(512), full(512), full(LANE),
                  pl.BlockSpec((LANE, LANE), lambda b, i: (0, 0))],
        out_specs=pl.BlockSpec((None, tq, 512), lambda b, i: (b, i, 0)),
        out_shape=jax.ShapeDtypeStruct((bsz, t, BRANCH_W), F32),
        scratch_shapes=[pltpu.VMEM((tq, t), F32), pltpu.VMEM((tq, t), F32)],
        compiler_params=_cp(2),
        name="dsa_prompt",
    )(h3, h3, h3, h3, cos, sin, ci, sia, sib, k16, v16, ki16, _tri128())


def _mem_kernel(q_ref, z_ref, mk_ref, mv_ref, o_ref):
    for h in range(N_HEADS):
        sl = slice(h * HEAD_DIM, (h + 1) * HEAD_DIM)
        lg = _nt(_b(q_ref[:, sl]), _b(mk_ref[:, sl])) * SCALE
        m = jnp.max(lg, -1, keepdims=True)
        p = jnp.exp(lg - m)
        l = jnp.sum(p, -1, keepdims=True)
        o = _mm(_b(p), _b(mv_ref[:, sl])) / l
        o_ref[:, sl] = o * _silu(z_ref[:, sl])


def _mem(h3, mkv3, tq):
    bsz, t, _ = h3.shape
    nm = mkv3.shape[1]
    return pl.pallas_call(
        _mem_kernel,
        grid=(bsz, t // tq),
        in_specs=[pl.BlockSpec((None, tq, 512), lambda b, i: (b, i, C_MEM // 512)),
                  pl.BlockSpec((None, tq, 512), lambda b, i: (b, i, C_MEM // 512 + 1)),
                  pl.BlockSpec((None, nm, 512), lambda b, i: (b, 0, 0)),
                  pl.BlockSpec((None, nm, 512), lambda b, i: (b, 0, 1))],
        out_specs=pl.BlockSpec((None, tq, 512), lambda b, i: (b, i, 0)),
        out_shape=jax.ShapeDtypeStruct((bsz, t, BRANCH_W), F32),
        compiler_params=_cp(2),
        name="mem_attn",
    )(h3, h3, mkv3, mkv3)


def _merge_kernel(ya_ref, yb_ref, yc_ref, yd_ref, ye_ref, g_ref, w_ref, o_ref):
    acc = None
    for nbr, y_ref in enumerate((ya_ref, yb_ref, yc_ref, yd_ref, ye_ref)):
        p = _mm(_b(y_ref[...]), w_ref[nbr])
        g = jax.nn.sigmoid(g_ref[:, nbr * D_MODEL:(nbr + 1) * D_MODEL].astype(F32))
        acc = g * p if acc is None else acc + g * p
    o_ref[...] = _b(acc)


def _merge(ys, g, w_br, tm):
    m = g.shape[0]
    ysp = pl.BlockSpec((tm, BRANCH_W), lambda i: (i, 0))
    return pl.pallas_call(
        _merge_kernel,
        grid=(m // tm,),
        in_specs=[ysp] * N_BRANCH + [pl.BlockSpec((tm, N_GATE), lambda i: (i, 0)),
                                     pl.BlockSpec((N_BRANCH, BRANCH_W, D_MODEL), lambda i: (0, 0, 0))],
        out_specs=pl.BlockSpec((tm, D_MODEL), lambda i: (i, 0)),
        out_shape=jax.ShapeDtypeStruct((m, D_MODEL), BF16),
        compiler_params=_cp(1),
        name="gated_merge",
    )(*ys, g, w_br)


def _outnorm_kernel(m_ref, x_ref, w_ref, g_ref, b_ref, y_ref, y16_ref):
    v = DEEPNORM_ALPHA * x_ref[...] + _mm(m_ref[...], w_ref[...])
    mu = jnp.mean(v, -1, keepdims=True)
    d = v - mu
    var = jnp.mean(d * d, -1, keepdims=True)
    y = d * lax.rsqrt(var + LN_EPS) * g_ref[...] + b_ref[...]
    y_ref[...] = y
    y16_ref[...] = _b(y)


def _outnorm(merged, x, w_out, ln_g, ln_b, tm):
    m = x.shape[0]
    row = pl.BlockSpec((tm, D_MODEL), lambda i: (i, 0))
    vec = pl.BlockSpec((1, D_MODEL), lambda i: (0, 0))
    return pl.pallas_call(
        _outnorm_kernel,
        grid=(m // tm,),
        in_specs=[row, row, pl.BlockSpec((D_MODEL, D_MODEL), lambda i: (0, 0)), vec, vec],
        out_specs=[row, row],
        out_shape=[jax.ShapeDtypeStruct((m, D_MODEL), F32), jax.ShapeDtypeStruct((m, D_MODEL), BF16)],
        compiler_params=_cp(1),
        name="out_norm",
    )(merged, x, w_out, ln_g, ln_b)


def _layer_params(l, w_in, conv_w, conv_b, gdn_a_log, gdn_dt_bias, gdn_norm_w, lru_lambda, lru_w_r, lru_b_r,
                  lru_w_i, lru_b_i, w_mem_kv, w_branch, w_out, ln_g, ln_b):
    w_main, w_gate = _prep_w_in(w_in, l)
    rows = jnp.stack([jnp.broadcast_to(gdn_a_log[l][:, None], (N_HEADS, LANE)),
                      jnp.broadcast_to(gdn_dt_bias[l][:, None], (N_HEADS, LANE)),
                      jnp.broadcast_to(gdn_norm_w[l][None, :], (N_HEADS, LANE))], 1)
    gdn_par = jnp.concatenate([rows, jnp.zeros((N_HEADS, 5, LANE), F32)], 1)
    return dict(
        w_main=w_main, w_gate=w_gate, conv_w=conv_w[l], conv_b=conv_b[l][None, :], gdn_par=gdn_par,
        ret_par=_ret_par(), lru_w_r=_b(lru_w_r[l]), lru_w_i=_b(lru_w_i[l]),
        lru_b_r=lru_b_r[l][None, :], lru_b_i=lru_b_i[l][None, :], lam=lru_lambda[l][None, :],
        w_mem_kv=_b(w_mem_kv[l]), w_branch=_b(w_branch[l]), w_out=_b(w_out[l]),
        ln_g=ln_g[l][None, :], ln_b=ln_b[l][None, :])


def _hist8(buf):
    return jnp.concatenate([jnp.zeros((buf.shape[0], 8 - (CONV_W - 1), buf.shape[2]), buf.dtype), buf], 1)


def _prompt_layer(x, x16, mem16, p, tabs):
    bsz, t, _ = x.shape
    m = bsz * t
    tm = math.gcd(m, 1024)
    h = _matmul_nt(x16, p["w_main"], F32, tm, 512)
    g = _matmul_nt(x16, p["w_gate"], BF16, tm, 1024)
    mkv = _matmul(mem16, p["w_mem_kv"], F32, math.gcd(mem16.shape[0], 512), 512)
    h3 = h.reshape(bsz, t, N_MAIN)
    mkv3 = mkv.reshape(bsz, -1, 2 * BRANCH_W)
    zero_hist = jnp.zeros((bsz, 8, CONV_DIM), F32)
    zero_state = jnp.zeros((bsz, N_HEADS, HEAD_DIM, HEAD_DIM), F32)
    cos, sin = tabs[0], tabs[1]
    y_a, s_gdn = _gdn(h3, zero_hist, p["conv_w"], p["conv_b"], p["gdn_par"], zero_state)
    kb, k16, v16, kif, ki16 = _rope_k(h3, tabs, math.gcd(t, 512))
    y_b = _dsa(h3, k16, v16, ki16, tabs, math.gcd(t, 256))
    y_c, s_ret = _ret(h3, cos, sin, p["ret_par"], zero_state)
    y_d, s_lru = _lru(h3, zero_hist, p["conv_w"], p["conv_b"], p["lru_w_r"], p["lru_w_i"], p["lru_b_r"],
                      p["lru_b_i"], p["lam"], jnp.zeros((bsz, 1, BRANCH_W), F32))
    y_e = _mem(h3, mkv3, math.gcd(t, 512))
    ys = [a.reshape(m, BRANCH_W) for a in (y_a, y_b, y_c, y_d, y_e)]
    merged = _merge(ys, g, p["w_branch"], math.gcd(m, 256))
    y, y16 = _outnorm(merged, x.reshape(m, D_MODEL), p["w_out"], p["ln_g"], p["ln_b"], math.gcd(m, 512))
    new_conv = h3[:, t - (CONV_W - 1):, C_CONV:C_CONV + CONV_DIM]
    v_rows = h3[:, :, C_DSA + 1024:C_DSA + 1536].reshape(bsz, t, N_HEADS, HEAD_DIM)
    mk = mkv3[:, :, :BRANCH_W].reshape(bsz, -1, N_HEADS, HEAD_DIM)
    mv = mkv3[:, :, BRANCH_W:].reshape(bsz, -1, N_HEADS, HEAD_DIM)
    return (y.reshape(bsz, t, D_MODEL), y16,
            (new_conv, s_gdn, s_ret, s_lru.reshape(bsz, BRANCH_W)),
            (kb.reshape(bsz, t, N_HEADS, HEAD_DIM), v_rows, kif[:, :, :IDX_DIM]), mk, mv)


def _dec_kernel(hs_ref, hist_ref, cw_ref, cb_ref, gpar_ref, rpar_ref, sg_ref, sr_ref, hl_ref,
                wr_ref, wi_ref, br_ref, bi_ref, lam_ref, mk_ref, mv_ref,
                cos_ref, sin_ref, ci_ref, sia_ref, sib_ref,
                ya_ref, yc_ref, yd_ref, ye_ref, sgo_ref, sro_ref, hlo_ref, qb_ref, kb_ref, qi_ref, ki_ref, ss_ref,
                wk_s, vb_s, qd_s, qkg_s, eg_s, kg_s, qx_s, qkr_s, kr_s, vr_s, mq_s, oa_s, oc_s, oe_s, *, nb):
    hd = HEAD_DIM
    hsl = lambda h: slice(h * hd, (h + 1) * hd)
    x_in = hs_ref[:, C_CONV:C_CONV + CONV_DIM]
    conv = cb_ref[...] + x_in * cw_ref[CONV_W - 1:CONV_W, :]
    for j in range(CONV_W - 1):
        conv = conv + hist_ref[j] * cw_ref[j:j + 1, :]
    sm = hs_ref[:, C_SMALL:C_SMALL + LANE]
    cos, sin = cos_ref[...], sin_ref[...]
    ci, sia, sib = ci_ref[...], sia_ref[...], sib_ref[...]
    ones = jnp.ones((1, hd), F32)
    for h in range(N_HEADS):
        sl = hsl(h)
        q = _silu(conv[:, h * hd:(h + 1) * hd])
        k = _silu(conv[:, BRANCH_W + h * hd:BRANCH_W + (h + 1) * hd])
        v = _silu(conv[:, 2 * BRANCH_W + h * hd:2 * BRANCH_W + (h + 1) * hd])
        q = q * lax.rsqrt(jnp.sum(q * q, -1, keepdims=True) + NORM_EPS) * SCALE
        k = k * lax.rsqrt(jnp.sum(k * k, -1, keepdims=True) + NORM_EPS)
        beta = jax.nn.sigmoid(sm[:, N_HEADS + h:N_HEADS + h + 1])
        g = -jnp.exp(gpar_ref[h, 0:1, 0:1]) * jax.nn.softplus(sm[:, h:h + 1] + gpar_ref[h, 1:2, 0:1])
        eg = jnp.exp(g)
        wk_s[h] = k * (beta * eg)
        vb_s[h] = v * beta
        qd_s[h] = q * eg
        qkg_s[h] = jnp.sum(q * k, -1, keepdims=True) * ones
        eg_s[h] = eg * ones
        kg_s[h] = k
        q = _rope128(hs_ref[:, C_RET + h * hd:C_RET + (h + 1) * hd], cos, sin)
        k = _rope128(hs_ref[:, C_RET + BRANCH_W + h * hd:C_RET + BRANCH_W + (h + 1) * hd], cos, sin) * SCALE
        qx_s[h] = q * rpar_ref[h, 0:1, 0:1]
        qkr_s[h] = jnp.sum(q * k, -1, keepdims=True) * ones
        kr_s[h] = k
        vr_s[h] = hs_ref[:, C_RET + 2 * BRANCH_W + h * hd:C_RET + 2 * BRANCH_W + (h + 1) * hd]
        mq_s[h] = hs_ref[:, C_MEM + h * hd:C_MEM + (h + 1) * hd]
        qb_ref[:, sl] = _rope128(hs_ref[:, C_DSA + h * hd:C_DSA + (h + 1) * hd], cos, sin)
        kb_ref[:, sl] = _rope128(hs_ref[:, C_DSA + BRANCH_W + h * hd:C_DSA + BRANCH_W + (h + 1) * hd], cos, sin)
    ki = _rope_idx(hs_ref[:, C_KI:C_KI + LANE], ci, sia, sib)
    ki_ref[...] = ki
    sself = jnp.zeros((nb, 1), F32)
    for h in range(IDX_HEADS):
        qi = _rope_idx(hs_ref[:, C_QI + h * LANE:C_QI + (h + 1) * LANE], ci, sia, sib)
        qi_ref[:, h * LANE:(h + 1) * LANE] = qi
        s = jnp.sum(qi * ki, -1, keepdims=True)
        sself = sself + jnp.maximum(s, 0.0) * (sm[:, 8 + h:9 + h] * IDX_SCALE)
    ss_ref[...] = sself * jnp.ones((1, LANE), F32)

    ii = lax.broadcasted_iota(jnp.int32, (hd, hd), 0)
    jj = lax.broadcasted_iota(jnp.int32, (hd, hd), 1)
    eye = jnp.where(ii == jj, 1.0, 0.0).astype(F32)

    def col_bcast(row):
        return _nt(eye, jnp.broadcast_to(row, (hd, hd)), HI)

    def per_batch(bb, carry):
        r = pl.ds(bb, 1)
        for h in range(N_HEADS):
            sl = hsl(h)
            s = sg_ref[bb, h]
            v_new = vb_s[h, r, :] - _mm(wk_s[h, r, :], s, HI)
            oa_s[h, r, :] = _mm(qd_s[h, r, :], s, HI) + qkg_s[h, r, :] * v_new
            sgo_ref[bb, h] = s * eg_s[h, r, :] + col_bcast(kg_s[h, r, :]) * v_new
            rs = sr_ref[bb, h]
            v = vr_s[h, r, :]
            oc_s[h, r, :] = _mm(qx_s[h, r, :], rs, HI) + qkr_s[h, r, :] * v
            sro_ref[bb, h] = rs * rpar_ref[h, 0:1, :] + col_bcast(kr_s[h, r, :]) * v
            lg = _nt(_b(mq_s[h, r, :]), _b(mk_ref[bb, :, sl])) * SCALE
            m = jnp.max(lg, -1, keepdims=True)
            p = jnp.exp(lg - m)
            oe_s[h, r, :] = _mm(_b(p), _b(mv_ref[bb, :, sl])) / jnp.sum(p, -1, keepdims=True)
        return carry

    lax.fori_loop(0, nb, per_batch, 0)

    for h in range(N_HEADS):
        sl = hsl(h)
        o = oa_s[h]
        o = o * lax.rsqrt(jnp.mean(o * o, -1, keepdims=True) + NORM_EPS) * gpar_ref[h, 2:3, :]
        ya_ref[:, sl] = o * _silu(hs_ref[:, C_GDNZ + h * hd:C_GDNZ + (h + 1) * hd])
        o = oc_s[h]
        d = o - jnp.mean(o, -1, keepdims=True)
        var = jnp.mean(d * d, -1, keepdims=True)
        yc_ref[:, sl] = d * lax.rsqrt(var + LN_EPS) * _silu(hs_ref[:, C_RET + 3 * BRANCH_W + h * hd:C_RET + 3 * BRANCH_W + (h + 1) * hd])
        ye_ref[:, sl] = oe_s[h] * _silu(hs_ref[:, C_MEM + BRANCH_W + h * hd:C_MEM + BRANCH_W + (h + 1) * hd])
    a, b = _lru_gates(conv[:, 3 * BRANCH_W:], wr_ref, wi_ref, br_ref[...], bi_ref[...], lam_ref[...])
    h_new = a * hl_ref[...] + b
    hlo_ref[...] = h_new
    yd_ref[...] = h_new * _silu(hs_ref[:, C_LRUZ:C_LRUZ + BRANCH_W])


def _decode(hs, hist_t, p, sg, sr, hl, mk3, mv3, tabs):
    nb = hs.shape[0]
    bw = jax.ShapeDtypeStruct((nb, BRANCH_W), F32)
    st = jax.ShapeDtypeStruct((nb, N_HEADS, HEAD_DIM, HEAD_DIM), F32)
    out_shape = [bw, bw, bw, bw, st, st, bw, bw, bw,
                 jax.ShapeDtypeStruct((nb, IDX_HEADS * LANE), F32), jax.ShapeDtypeStruct((nb, LANE), F32),
                 jax.ShapeDtypeStruct((nb, LANE), F32)]
    return pl.pallas_call(
        functools.partial(_dec_kernel, nb=nb),
        out_shape=out_shape,
        scratch_shapes=[pltpu.VMEM((N_HEADS, nb, HEAD_DIM), F32)] * 14,
        compiler_params=pltpu.CompilerParams(vmem_limit_bytes=VMEM_LIMIT),
        name="decode_step",
    )(hs, hist_t, p["conv_w"], p["conv_b"], p["gdn_par"], p["ret_par"], sg, sr, hl,
      p["lru_w_r"], p["lru_w_i"], p["lru_b_r"], p["lru_b_i"], p["lam"], mk3, mv3, *tabs)


PAGES_PER_STEP = 8


def _didx_kernel(pt_ref, qi_ref, w_ref, *rest):
    pages, o_ref = rest[:PAGES_PER_STEP], rest[PAGES_PER_STEP]
    q16 = _b(qi_ref[:, :IDX_DIM])
    w = w_ref[...]
    for j, pg in enumerate(pages):
        s = _mm(q16, _b(pg[...]))
        o_ref[j:j + 1, :] = jnp.sum(jnp.maximum(s, 0.0) * w, 0, keepdims=True)


def _didx(page_table, qi3, wcol, pool_ikt, l):
    nb, n_pages = page_table.shape
    ng = n_pages // PAGES_PER_STEP
    page = lambda j: pl.BlockSpec((None, None, IDX_DIM, PAGE_SIZE),
                                  lambda b, g, pt, j=j: (l, pt[b, g * PAGES_PER_STEP + j], 0, 0))
    return pl.pallas_call(
        _didx_kernel,
        grid_spec=pltpu.PrefetchScalarGridSpec(
            num_scalar_prefetch=1, grid=(nb, ng),
            in_specs=[pl.BlockSpec((None, IDX_HEADS, LANE), lambda b, g, pt: (b, 0, 0)),
                      pl.BlockSpec((None, IDX_HEADS, 1), lambda b, g, pt: (b, 0, 0))]
            + [page(j) for j in range(PAGES_PER_STEP)],
            out_specs=pl.BlockSpec((None, None, PAGES_PER_STEP, PAGE_SIZE), lambda b, g, pt: (b, g, 0, 0))),
        out_shape=jax.ShapeDtypeStruct((nb, ng, PAGES_PER_STEP, PAGE_SIZE), F32),
        compiler_params=_cp(2),
        name="paged_indexer",
    )(page_table, qi3, wcol, *([pool_ikt] * PAGES_PER_STEP))


NEG_SEL = -2e30
M_INIT = -1e30


def _dsel_kernel(sc_ref, ss_ref, tri_ref, bias_ref, sbias_ref, *, n_pages, ksel):
    key = sc_ref[...]
    kself = ss_ref[:, 0:1]

    def reduce2(x, op):
        return op(op(x, 1, keepdims=True), 0, keepdims=True)

    def count_ge(th):
        return reduce2(jnp.where(key >= th, 1.0, 0.0), jnp.sum) + jnp.where(kself >= th, 1.0, 0.0)

    def count_gt(th):
        return reduce2(jnp.where(key > th, 1.0, 0.0), jnp.sum) + jnp.where(kself > th, 1.0, 0.0)

    def next_above(th):
        return jnp.minimum(reduce2(jnp.where(key > th, key, jnp.inf), jnp.min), jnp.where(kself > th, kself, jnp.inf))

    lo = jnp.minimum(reduce2(key, jnp.min), kself)
    hi = jnp.maximum(reduce2(key, jnp.max), kself)
    thr, n_gt = _kth_threshold(count_ge, count_gt, next_above, lo, hi, float(ksel))
    eq = key == thr
    need = ksel - n_gt
    pre_in = _mm(jnp.where(eq, 1.0, 0.0).astype(BF16), tri_ref[...])
    tot = pre_in[:, PAGE_SIZE - 1:PAGE_SIZE]
    ii = lax.broadcasted_iota(jnp.int32, (n_pages, n_pages), 0)
    jj = lax.broadcasted_iota(jnp.int32, (n_pages, n_pages), 1)
    below = jnp.where(ii > jj, 1.0, 0.0).astype(BF16)
    pre = pre_in + _mm(below, _b(jnp.broadcast_to(tot, (n_pages, PAGE_SIZE))))
    bias_ref[...] = jnp.where(key > thr, 0.0, jnp.where(eq, jnp.where(pre <= need, 0.0, NEG_SEL), NEG_SEL))
    n_eq = pre[n_pages - 1:n_pages, PAGE_SIZE - 1:PAGE_SIZE]
    sb = jnp.where(kself > thr, 0.0, jnp.where(kself == thr, jnp.where(n_eq + 1.0 <= need, 0.0, NEG_SEL), NEG_SEL))
    sbias_ref[...] = sb * jnp.ones((1, LANE), F32)


def _dsel(scores3, sself3):
    nb, n_pages, _ = scores3.shape
    ksel = min(DSA_TOPK, (n_pages * PAGE_SIZE + 1) // 4)
    return pl.pallas_call(
        functools.partial(_dsel_kernel, n_pages=n_pages, ksel=ksel),
        grid=(nb,),
        in_specs=[pl.BlockSpec((None, n_pages, PAGE_SIZE), lambda b: (b, 0, 0)),
                  pl.BlockSpec((None, 1, LANE), lambda b: (b, 0, 0)),
                  pl.BlockSpec((LANE, LANE), lambda b: (0, 0))],
        out_specs=[pl.BlockSpec((None, n_pages, PAGE_SIZE), lambda b: (b, 0, 0)),
                   pl.BlockSpec((None, 1, LANE), lambda b: (b, 0, 0))],
        out_shape=[jax.ShapeDtypeStruct((nb, n_pages, PAGE_SIZE), F32), jax.ShapeDtypeStruct((nb, 1, LANE), F32)],
        compiler_params=_cp(1),
        name="sample_select",
    )(scores3, sself3, _tri128())


def _datt_kernel(pt_ref, q_ref, bias_ref, sbias_ref, kself_ref, vself_ref, z_ref, *rest, ng):
    kpages = rest[:PAGES_PER_STEP]
    vpages = rest[PAGES_PER_STEP:2 * PAGES_PER_STEP]
    o_ref, m_s, l_s, acc_s = rest[2 * PAGES_PER_STEP:]
    g = pl.program_id(1)

    @pl.when(g == 0)
    def _():
        m_s[...] = jnp.full(m_s.shape, M_INIT, F32)
        l_s[...] = jnp.zeros(l_s.shape, F32)
        acc_s[...] = jnp.zeros(acc_s.shape, F32)

    rows = PAGE_SIZE * N_HEADS
    sub = lax.broadcasted_iota(jnp.int32, (8, rows), 0)
    own = lax.broadcasted_iota(jnp.int32, (8, rows), 1) % N_HEADS == sub
    q8 = q_ref[...]
    q16 = _b(q8)

    def update(lg, pv):
        m_prev = m_s[...]
        m_new = jnp.maximum(m_prev, jnp.max(lg, -1, keepdims=True))
        alpha = jnp.exp(m_prev - m_new)
        p = jnp.exp(lg - m_new)
        l_s[...] = l_s[...] * alpha + jnp.sum(p, -1, keepdims=True)
        acc_s[...] = acc_s[...] * alpha + pv(p)
        m_s[...] = m_new

    lg = jnp.concatenate(
        [jnp.where(own, _nt(q16, _b(kpages[j][...])) * SCALE + bias_ref[j:j + 1, :], NEG_SEL)
         for j in range(PAGES_PER_STEP)], 1)

    def pv_pages(p):
        acc = _mm(_b(p[:, 0:rows]), _b(vpages[0][...]))
        for j in range(1, PAGES_PER_STEP):
            acc = acc + _mm(_b(p[:, j * rows:(j + 1) * rows]), _b(vpages[j][...]))
        return acc

    update(lg, pv_pages)

    @pl.when(g == ng - 1)
    def _():
        lgs = jnp.sum(q8 * kself_ref[...], -1, keepdims=True) * SCALE + sbias_ref[:, 0:1]
        update(lgs, lambda p: p * vself_ref[...])
        l = l_s[...]
        o_ref[...] = acc_s[...] / jnp.where(l > 0.0, l, 1.0) * _silu(z_ref[...])


def _datt(page_table, q8, bias4, sbias3, kself8, vself8, z8, pool_k, pool_v, l):
    nb, n_pages = page_table.shape
    ng = n_pages // PAGES_PER_STEP
    rows = PAGE_SIZE * N_HEADS
    head = pl.BlockSpec((None, 8, HEAD_DIM), lambda b, g, pt: (b, 0, 0))
    page = lambda j: pl.BlockSpec((None, None, rows, HEAD_DIM),
                                  lambda b, g, pt, j=j: (l, pt[b, g * PAGES_PER_STEP + j], 0, 0))
    return pl.pallas_call(
        functools.partial(_datt_kernel, ng=ng),
        grid_spec=pltpu.PrefetchScalarGridSpec(
            num_scalar_prefetch=1, grid=(nb, ng),
            in_specs=[head, pl.BlockSpec((None, PAGES_PER_STEP, rows), lambda b, g, pt: (b, g, 0)),
                      pl.BlockSpec((None, 1, LANE), lambda b, g, pt: (b, 0, 0)), head, head, head]
            + [page(j) for j in range(PAGES_PER_STEP)] * 2,
            out_specs=head,
            scratch_shapes=[pltpu.VMEM((8, 1), F32), pltpu.VMEM((8, 1), F32), pltpu.VMEM((8, HEAD_DIM), F32)]),
        out_shape=jax.ShapeDtypeStruct((nb, 8, HEAD_DIM), F32),
        compiler_params=_cp(2),
        name="paged_attention",
    )(page_table, q8, bias4, sbias3, kself8, vself8, z8, *([pool_k] * PAGES_PER_STEP), *([pool_v] * PAGES_PER_STEP))


def _sample_layer(x, x16, p, tabs, conv_buf, sg, sr, hl, mem_k, mem_v, pool_k, pool_v, pool_ik, page_table, l):
    nb = x.shape[0]
    hs = _matmul_nt(x16, p["w_main"], F32, nb, 512)
    gs = _matmul_nt(x16, p["w_gate"], BF16, nb, 1024)
    mk3 = mem_k.reshape(nb, -1, BRANCH_W)
    mv3 = mem_v.reshape(nb, -1, BRANCH_W)
    (y_a, y_c, y_d, y_e, sg_new, sr_new, hl_new, qb, kb, qi, ki, sself) = _decode(
        hs, conv_buf.transpose(1, 0, 2), p, sg, sr, hl, mk3, mv3, tabs)
    wcol = (hs[:, C_SMALL + 8:C_SMALL + 16] * IDX_SCALE)[:, :, None]
    scores = _didx(page_table, qi.reshape(nb, IDX_HEADS, LANE), wcol, jnp.swapaxes(pool_ik, 2, 3), l)
    n_pages = page_table.shape[1]
    bias3, sbias3 = _dsel(scores.reshape(nb, n_pages, PAGE_SIZE), sself[:, None, :])
    v_self = hs[:, C_DSA + 1024:C_DSA + 1536]

    def heads8(a):
        a = a.reshape(nb, N_HEADS, HEAD_DIM)
        return jnp.concatenate([a, jnp.zeros((nb, 8 - N_HEADS, HEAD_DIM), a.dtype)], 1)

    pages = lambda pool: pool.reshape(*pool.shape[:2], PAGE_SIZE * N_HEADS, HEAD_DIM)
    y_b = _datt(page_table, heads8(qb), jnp.repeat(bias3, N_HEADS, axis=-1), sbias3, heads8(kb), heads8(v_self),
                heads8(hs[:, C_DSA + 1536:C_DSA + 2048]), pages(pool_k), pages(pool_v), l)
    merged = _merge([y_a, y_b[:, :N_HEADS].reshape(nb, BRANCH_W), y_c, y_d, y_e], gs, p["w_branch"], nb)
    y, y16 = _outnorm(merged, x.reshape(nb, D_MODEL), p["w_out"], p["ln_g"], p["ln_b"], nb)
    new_conv = jnp.concatenate([conv_buf[:, 1:], hs[:, None, C_CONV:C_CONV + CONV_DIM]], 1)
    return (y.reshape(nb, 1, D_MODEL), y16, (new_conv, sg_new, sr_new, hl_new),
            (kb.reshape(nb, 1, N_HEADS, HEAD_DIM), v_self.reshape(nb, 1, N_HEADS, HEAD_DIM), ki[:, None, :IDX_DIM]))


def kernel(x_prompt, x_sample, cache_conv, cache_dsa_k, cache_dsa_v, cache_dsa_idx_k, cache_mem_k, cache_mem_v,
           state_gdn, state_ret, state_lru, page_table, mem_prompt, w_in, conv_w, conv_b, gdn_a_log, gdn_dt_bias,
           gdn_norm_w, lru_lambda, lru_w_r, lru_b_r, lru_w_i, lru_b_i, w_mem_kv, w_branch, w_out, ln_g, ln_b):
    bsz, t, _ = x_prompt.shape
    nb = x_sample.shape[0]
    depth = w_in.shape[0]
    past = page_table.shape[1] * PAGE_SIZE
    tabs_p = _rope_tables(np.arange(t))
    tabs_s = _rope_tables(np.asarray([past]))
    mem16 = _b(mem_prompt.reshape(-1, D_MODEL))
    xp, xs = x_prompt, x_sample
    xp16, xs16 = _b(xp.reshape(-1, D_MODEL)), _b(xs.reshape(-1, D_MODEL))
    acc_p, acc_s = [], []
    for l in range(depth):
        p = _layer_params(l, w_in, conv_w, conv_b, gdn_a_log, gdn_dt_bias, gdn_norm_w, lru_lambda, lru_w_r, lru_b_r,
                          lru_w_i, lru_b_i, w_mem_kv, w_branch, w_out, ln_g, ln_b)
        xp, xp16, st, rows, mk, mv = _prompt_layer(xp, xp16, mem16, p, tabs_p)
        acc_p.append(st + rows + (mk, mv))
        xs, xs16, st, rows = _sample_layer(xs, xs16, p, tabs_s, cache_conv[l], state_gdn[l], state_ret[l], state_lru[l],
                                           cache_mem_k[l], cache_mem_v[l], cache_dsa_k, cache_dsa_v,
                                           cache_dsa_idx_k, page_table, l)
        acc_s.append(st + rows)
    stack = lambda acc, i: jnp.stack([a[i] for a in acc])
    return (xp, xs) + tuple(stack(acc_p, i) for i in range(9)) + tuple(stack(acc_s, i) for i in range(7))
```

```python
import functools
import math

import numpy as np
import jax
import jax.numpy as jnp
from jax import lax
from jax.experimental import pallas as pl
from jax.experimental.pallas import tpu as pltpu

F32 = jnp.float32
BF16 = jnp.bfloat16
HI = lax.Precision.HIGHEST

D_MODEL = 2048
HEAD_DIM = 128
BRANCH_W = 512
N_HEADS = 4
N_BRANCH = 5
CONV_W = 4
CONV_DIM = 2048
GDN_CHUNK = 64
GDN_GROUP = 256
RET_CHUNK = 256
DSA_TOPK = 256
IDX_HEADS = 8
IDX_DIM = 64
LRU_C = 8.0
ROPE_THETA = 10000.0
PAGE_SIZE = 128
LN_EPS = 1e-5
NORM_EPS = 1e-6
DEPTH = 2
DEEPNORM_ALPHA = (2 * DEPTH) ** 0.25
IN_SPLITS = (CONV_DIM, BRANCH_W, N_HEADS, N_HEADS,
             BRANCH_W, BRANCH_W, BRANCH_W, BRANCH_W, IDX_HEADS * IDX_DIM, IDX_DIM, IDX_HEADS,
             BRANCH_W, BRANCH_W, BRANCH_W, BRANCH_W,
             BRANCH_W,
             BRANCH_W, BRANCH_W,
             N_BRANCH * D_MODEL)

LANE = 128
C_CONV = 0
C_GDNZ = 2048
C_DSA = 2560
C_RET = 4608
C_LRUZ = 6656
C_MEM = 7168
C_QI = 8192
C_KI = 9216
C_SMALL = 9344
N_MAIN = 9728
N_GATE = N_BRANCH * D_MODEL
SCALE = HEAD_DIM ** -0.5
IDX_SCALE = (IDX_HEADS * IDX_DIM) ** -0.5
NEG = -1e30
VMEM_LIMIT = 56 * 1024 * 1024


def _cp(n_grid):
    return pltpu.CompilerParams(dimension_semantics=("arbitrary",) * n_grid, vmem_limit_bytes=VMEM_LIMIT)


def _b(x):
    return x.astype(BF16)


def _mm(a, b, precision=None):
    return jnp.dot(a, b, precision=precision, preferred_element_type=F32)


def _nt(a, b, precision=None):
    return lax.dot_general(a, b, (((1,), (1,)), ((), ())), precision=precision, preferred_element_type=F32)


def _tn(a, b, precision=None):
    return lax.dot_general(a, b, (((0,), (0,)), ((), ())), precision=precision, preferred_element_type=F32)


def _split3(x):
    hi = _b(x)
    r = x - hi.astype(F32)
    mid = _b(r)
    return hi, mid, _b(r - mid.astype(F32))


def _dot_pieces(dot, sel16, pieces):
    hi, mid, lo = pieces
    return (dot(sel16, lo) + dot(sel16, mid)) + dot(sel16, hi)


def _silu(x):
    return x * jax.nn.sigmoid(x)


def _rope128(x, cos, sin):
    return x * cos + pltpu.roll(x, 64, 1) * sin


def _rope_idx(x, cos, sina, sinb):
    return x * cos + pltpu.roll(x, 96, 1) * sina + pltpu.roll(x, 32, 1) * sinb


def _rope_tables(pos):
    pos = np.asarray(pos, np.float64)[:, None]
    ang = pos * (ROPE_THETA ** (-np.arange(64, dtype=np.float64) / 64))[None, :]
    c, s = np.cos(ang), np.sin(ang)
    cos128 = np.concatenate([c, c], 1)
    sin128 = np.concatenate([-s, s], 1)
    ang32 = pos * (ROPE_THETA ** (-np.arange(32, dtype=np.float64) / 32))[None, :]
    c, s = np.cos(ang32), np.sin(ang32)
    z = np.zeros_like(c)
    cos_i = np.concatenate([c, c, z, z], 1)
    sin_ia = np.concatenate([-s, z, z, z], 1)
    sin_ib = np.concatenate([z, s, z, z], 1)
    return tuple(jnp.asarray(a, F32) for a in (cos128, sin128, cos_i, sin_ia, sin_ib))


def _mm_kernel(x_ref, w_ref, o_ref):
    o_ref[...] = _mm(x_ref[...], w_ref[...]).astype(o_ref.dtype)


def _matmul(x, w, out_dtype, tm, tn):
    m, k = x.shape
    n = w.shape[1]
    return pl.pallas_call(
        _mm_kernel,
        grid=(m // tm, n // tn),
        in_specs=[pl.BlockSpec((tm, k), lambda i, j: (i, 0)),
                  pl.BlockSpec((k, tn), lambda i, j: (0, j))],
        out_specs=pl.BlockSpec((tm, tn), lambda i, j: (i, j)),
        out_shape=jax.ShapeDtypeStruct((m, n), out_dtype),
        compiler_params=_cp(2),
        name="proj_matmul",
    )(x, w)


def _mm_nt_kernel(x_ref, w_ref, o_ref):
    o_ref[...] = _nt(x_ref[...], w_ref[...]).astype(o_ref.dtype)


def _matmul_nt(x, wt, out_dtype, tm, tn):
    m, k = x.shape
    n = wt.shape[0]
    return pl.pallas_call(
        _mm_nt_kernel,
        grid=(m // tm, n // tn),
        in_specs=[pl.BlockSpec((tm, k), lambda i, j: (i, 0)),
                  pl.BlockSpec((tn, k), lambda i, j: (j, 0))],
        out_specs=pl.BlockSpec((tm, tn), lambda i, j: (i, j)),
        out_shape=jax.ShapeDtypeStruct((m, n), out_dtype),
        compiler_params=_cp(2),
        name="proj_matmul_nt",
    )(x, wt)


W_PREP_COLS = 128


def _relayout_kernel(w_ref, main_ref, gate_ref):
    offs = [int(o) for o in np.concatenate([[0], np.cumsum(IN_SPLITS)])]
    cols = main_ref.shape[1]

    def put(dst, src, n):
        main_ref[dst:dst + n, :] = _b(w_ref[src:src + n, :])

    def put_padded(dst, pieces):
        parts = [w_ref[src:src + n, :] for src, n in pieces]
        parts.append(jnp.zeros((LANE - sum(n for _, n in pieces), cols), F32))
        main_ref[dst:dst + LANE, :] = _b(jnp.concatenate(parts, 0))

    put(C_CONV, offs[0], CONV_DIM)
    put(C_GDNZ, offs[1], BRANCH_W)
    put(C_DSA, offs[4], 4 * BRANCH_W)
    put(C_RET, offs[11], 4 * BRANCH_W)
    put(C_LRUZ, offs[15], BRANCH_W)
    put(C_MEM, offs[16], 2 * BRANCH_W)
    for h in range(IDX_HEADS):
        put_padded(C_QI + h * LANE, [(offs[8] + h * IDX_DIM, IDX_DIM)])
    put_padded(C_KI, [(offs[9], IDX_DIM)])
    put_padded(C_SMALL, [(offs[2], 2 * N_HEADS), (offs[10], IDX_HEADS)])
    main_ref[C_SMALL + LANE:N_MAIN, :] = jnp.zeros((N_MAIN - C_SMALL - LANE, cols), BF16)
    gate_ref[...] = _b(w_ref[offs[18]:offs[18] + N_GATE, :])


def _prep_w_in(w_in, l):
    wt = jnp.swapaxes(w_in, 1, 2)
    _, n_in, d = wt.shape
    return pl.pallas_call(
        _relayout_kernel,
        grid=(d // W_PREP_COLS,),
        in_specs=[pl.BlockSpec((None, n_in, W_PREP_COLS), lambda i: (l, 0, i))],
        out_specs=[pl.BlockSpec((N_MAIN, W_PREP_COLS), lambda i: (0, i)),
                   pl.BlockSpec((N_GATE, W_PREP_COLS), lambda i: (0, i))],
        out_shape=[jax.ShapeDtypeStruct((N_MAIN, d), BF16), jax.ShapeDtypeStruct((N_GATE, d), BF16)],
        compiler_params=_cp(1),
        name="w_in_relayout",
    )(wt)


def _ropek_kernel(k_ref, v_ref, ki_ref, cos_ref, sin_ref, ci_ref, sia_ref, sib_ref,
                  kb_ref, k16_ref, v16_ref, kif_ref, ki16_ref):
    cos, sin = cos_ref[...], sin_ref[...]
    for h in range(N_HEADS):
        sl = slice(h * HEAD_DIM, (h + 1) * HEAD_DIM)
        r = _rope128(k_ref[:, sl], cos, sin)
        kb_ref[:, sl] = r
        k16_ref[:, sl] = _b(r)
    v16_ref[...] = _b(v_ref[...])
    ki = _rope_idx(ki_ref[...], ci_ref[...], sia_ref[...], sib_ref[...])
    kif_ref[...] = ki
    ki16_ref[...] = _b(ki)


def _rope_k(h3, tabs, tr):
    bsz, t, _ = h3.shape
    cb = lambda c, w: c // w
    tab = pl.BlockSpec((tr, LANE), lambda b, i: (i, 0))
    row = lambda w, c: pl.BlockSpec((None, tr, w), lambda b, i: (b, i, c))
    return pl.pallas_call(
        _ropek_kernel,
        grid=(bsz, t // tr),
        in_specs=[row(512, cb(C_DSA + 512, 512)), row(512, cb(C_DSA + 1024, 512)), row(LANE, cb(C_KI, LANE)),
                  tab, tab, tab, tab, tab],
        out_specs=[row(512, 0), row(512, 0), row(512, 0), row(LANE, 0), row(LANE, 0)],
        out_shape=[jax.ShapeDtypeStruct((bsz, t, 512), F32), jax.ShapeDtypeStruct((bsz, t, 512), BF16),
                   jax.ShapeDtypeStruct((bsz, t, 512), BF16), jax.ShapeDtypeStruct((bsz, t, LANE), F32),
                   jax.ShapeDtypeStruct((bsz, t, LANE), BF16)],
        compiler_params=_cp(2),
        name="rope_k",
    )(h3, h3, h3, *tabs)


def _gdn_kernel(q_ref, k_ref, v_ref, z_ref, sm_ref, wq_ref, wk_ref, wv_ref, bq_ref, bk_ref, bv_ref,
                hq_ref, hk_ref, hv_ref, par_ref, s0_ref, y_ref, sfin_ref,
                scq, sck, scv, o0_s, qe_s, n_s, p_s, gl_s, *, t, c, gr):
    h = pl.program_id(1)
    nc = t // c
    cpg = gr // c
    for src, hist, scr in ((q_ref, hq_ref, scq), (k_ref, hk_ref, sck), (v_ref, hv_ref, scv)):
        scr[pl.ds(0, 8), :] = hist[...]
        scr[pl.ds(8, t), :] = src[...]
    ii = lax.broadcasted_iota(jnp.int32, (gr, gr), 0)
    jj = lax.broadcasted_iota(jnp.int32, (gr, gr), 1)
    same = (ii // c) == (jj // c)
    tril = jnp.logical_and(same, ii >= jj)
    strict = jnp.logical_and(same, ii > jj)
    tril16 = jnp.where(tril, 1.0, 0.0).astype(BF16)
    same16 = jnp.where(same, 1.0, 0.0).astype(BF16)
    lane = lax.broadcasted_iota(jnp.int32, (gr, LANE), 1)
    e0 = jnp.where(lane == 0, 1.0, 0.0).astype(BF16)
    neg_a = -jnp.exp(par_ref[0:1, 0:1])
    dt_bias = par_ref[1:2, 0:1]
    norm_w = par_ref[2:3, :]

    def conv(scr, w_ref, b_ref, r0):
        acc = b_ref[...] + scr[pl.ds(r0 + 5, gr), :] * w_ref[0:1, :]
        for j in range(1, CONV_W):
            acc = acc + scr[pl.ds(r0 + 5 + j, gr), :] * w_ref[j:j + 1, :]
        return acc

    def prep(gi, carry):
        r0 = pl.multiple_of(gi * gr, gr)
        qc = _silu(conv(scq, wq_ref, bq_ref, r0))
        kc = _silu(conv(sck, wk_ref, bk_ref, r0))
        vc = _silu(conv(scv, wv_ref, bv_ref, r0))
        qn = qc * lax.rsqrt(jnp.sum(qc * qc, -1, keepdims=True) + NORM_EPS) * SCALE
        kn = kc * lax.rsqrt(jnp.sum(kc * kc, -1, keepdims=True) + NORM_EPS)
        sm = sm_ref[pl.ds(r0, gr), :]
        a_col = jnp.sum(jnp.where(lane == h, sm, 0.0), -1, keepdims=True)
        b_col = jnp.sum(jnp.where(lane == N_HEADS + h, sm, 0.0), -1, keepdims=True)
        beta = jax.nn.sigmoid(b_col)
        gb = _split3(jnp.broadcast_to(neg_a * jax.nn.softplus(a_col + dt_bias), (gr, LANE)))
        gcb = _dot_pieces(_mm, tril16, gb)
        glb = _dot_pieces(_mm, same16, gb)
        gc = gcb[:, 0:1]
        gcr = _dot_pieces(_nt, e0, _split3(gcb))
        decay = jnp.where(tril, jnp.exp(jnp.minimum(gc - gcr, 0.0)), 0.0)
        q16, k16 = _b(qn), _b(kn)
        a = jnp.where(strict, beta * _nt(k16, k16) * decay, 0.0)
        x = -a
        r = x
        for _ in range(int(math.log2(c)) - 1):
            x16 = _b(x)
            x = _mm(x16, x16)
            r = r + x + _mm(_b(r), _b(x))
        egc = jnp.exp(gc)
        r16 = _b(r)
        rhs_v = vc * beta
        rhs_k = kn * (beta * egc)
        u = rhs_v + _mm(r16, _b(rhs_v))
        w = rhs_k + _mm(r16, _b(rhs_k))
        qk = _nt(q16, k16) * decay
        qd = qn * egc
        kt = kn * jnp.exp(glb[:, 0:1] - gc)
        egl = jnp.exp(glb)
        qk16 = _b(qk)
        qe = qd - _mm(qk16, _b(w))
        o0 = _mm(qk16, _b(u))
        for j in range(cpg):
            ci = gi * cpg + j
            rows = slice(j * c, (j + 1) * c)
            o0_s[ci] = o0[rows]
            qe_s[ci] = _b(qe[rows])
            n_s[ci] = _tn(kt[rows], u[rows])
            p_s[ci] = _b(_tn(kt[rows], w[rows]))
            gl_s[ci] = egl[j * c:j * c + 1]
        return carry

    lax.fori_loop(0, t // gr, prep, 0, unroll=math.gcd(t // gr, 2))

    def step(ci, s):
        r0 = pl.multiple_of(ci * c, c)
        s16 = _b(s)
        o = o0_s[ci] + _mm(qe_s[ci], s16)
        s = s * gl_s[ci] + (n_s[ci] - _mm(p_s[ci], s16))
        o = o * lax.rsqrt(jnp.mean(o * o, -1, keepdims=True) + NORM_EPS) * norm_w
        y_ref[pl.ds(r0, c), :] = o * _silu(z_ref[pl.ds(r0, c), :])
        return s

    sfin_ref[...] = lax.fori_loop(0, nc, step, s0_ref[...])


def _gdn(h3, hist8, conv_w, conv_b, par, s0):
    bsz, t, _ = h3.shape
    c = math.gcd(t, GDN_CHUNK)
    gr = math.gcd(t, GDN_GROUP)
    nc = t // c
    col = lambda blk: pl.BlockSpec((None, t, LANE), lambda b, h, blk=blk: (b, 0, blk + h))
    wsp = lambda blk: pl.BlockSpec((CONV_W, LANE), lambda b, h, blk=blk: (0, blk + h))
    bsp = lambda blk: pl.BlockSpec((1, LANE), lambda b, h, blk=blk: (0, blk + h))
    hsp = lambda blk: pl.BlockSpec((None, 8, LANE), lambda b, h, blk=blk: (b, 0, blk + h))
    st = pl.BlockSpec((None, None, HEAD_DIM, HEAD_DIM), lambda b, h: (b, h, 0, 0))
    return pl.pallas_call(
        functools.partial(_gdn_kernel, t=t, c=c, gr=gr),
        grid=(bsz, N_HEADS),
        in_specs=[col(0), col(4), col(8), col(C_GDNZ // LANE),
                  pl.BlockSpec((None, t, LANE), lambda b, h: (b, 0, C_SMALL // LANE)),
                  wsp(0), wsp(4), wsp(8), bsp(0), bsp(4), bsp(8), hsp(0), hsp(4), hsp(8),
                  pl.BlockSpec((None, 8, LANE), lambda b, h: (h, 0, 0)), st],
        out_specs=[pl.BlockSpec((None, t, LANE), lambda b, h: (b, 0, h)), st],
        out_shape=[jax.ShapeDtypeStruct((bsz, t, BRANCH_W), F32),
                   jax.ShapeDtypeStruct((bsz, N_HEADS, HEAD_DIM, HEAD_DIM), F32)],
        scratch_shapes=[pltpu.VMEM((t + 8, LANE), F32)] * 3 + [
            pltpu.VMEM((nc, c, LANE), F32), pltpu.VMEM((nc, c, LANE), BF16),
            pltpu.VMEM((nc, HEAD_DIM, HEAD_DIM), F32), pltpu.VMEM((nc, HEAD_DIM, HEAD_DIM), BF16),
            pltpu.VMEM((nc, 1, LANE), F32)],
        compiler_params=_cp(2),
        name="gdn",
    )(h3, h3, h3, h3, h3, conv_w, conv_w, conv_w, conv_b, conv_b, conv_b, hist8, hist8, hist8, par, s0)


def _ret_kernel(q_ref, k_ref, v_ref, z_ref, cos_ref, sin_ref, par_ref, r0_ref, y_ref, rfin_ref, *, t, c):
    nc = t // c
    lg = jnp.log(par_ref[0:1, 0:1])
    ii = lax.broadcasted_iota(jnp.int32, (c, c), 0)
    jj = lax.broadcasted_iota(jnp.int32, (c, c), 1)
    dmat = jnp.where(ii >= jj, jnp.exp((ii - jj).astype(F32) * lg), 0.0)
    icol = lax.broadcasted_iota(jnp.int32, (c, 1), 0).astype(F32)
    xi = jnp.exp((icol + 1.0) * lg)
    zeta = jnp.exp((c - 1.0 - icol) * lg)
    g_c = jnp.exp(c * lg)

    def step(ci, r):
        rows = pl.ds(pl.multiple_of(ci * c, c), c)
        cos, sin = cos_ref[rows, :], sin_ref[rows, :]
        q = _rope128(q_ref[rows, :], cos, sin)
        k = _rope128(k_ref[rows, :], cos, sin) * SCALE
        v = v_ref[rows, :]
        q16, k16, v16 = _b(q), _b(k), _b(v)
        o = _mm(_b(_nt(q16, k16) * dmat), v16) + _mm(_b(q * xi), _b(r))
        r = r * g_c + _tn(k * zeta, v)
        mu = jnp.mean(o, -1, keepdims=True)
        d = o - mu
        var = jnp.mean(d * d, -1, keepdims=True)
        y_ref[rows, :] = d * lax.rsqrt(var + LN_EPS) * _silu(z_ref[rows, :])
        return r

    rfin_ref[...] = lax.fori_loop(0, nc, step, r0_ref[...], unroll=math.gcd(nc, CHUNK_UNROLL))


def _ret_par():
    gamma = (1.0 - 2.0 ** (-5.0 - np.arange(N_HEADS, dtype=np.float64))).astype(np.float32)
    return jnp.asarray(np.broadcast_to(gamma[:, None, None], (N_HEADS, 8, LANE)).copy(), F32)


def _ret(h3, cos, sin, par, r0):
    bsz, t, _ = h3.shape
    c = math.gcd(t, RET_CHUNK)
    col = lambda blk: pl.BlockSpec((None, t, LANE), lambda b, h, blk=blk: (b, 0, blk + h))
    tab = pl.BlockSpec((t, LANE), lambda b, h: (0, 0))
    st = pl.BlockSpec((None, None, HEAD_DIM, HEAD_DIM), lambda b, h: (b, h, 0, 0))
    base = C_RET // LANE
    return pl.pallas_call(
        functools.partial(_ret_kernel, t=t, c=c),
        grid=(bsz, N_HEADS),
        in_specs=[col(base), col(base + 4), col(base + 8), col(base + 12), tab, tab,
                  pl.BlockSpec((None, 8, LANE), lambda b, h: (h, 0, 0)), st],
        out_specs=[pl.BlockSpec((None, t, LANE), lambda b, h: (b, 0, h)), st],
        out_shape=[jax.ShapeDtypeStruct((bsz, t, BRANCH_W), F32),
                   jax.ShapeDtypeStruct((bsz, N_HEADS, HEAD_DIM, HEAD_DIM), F32)],
        compiler_params=_cp(2),
        name="retention",
    )(h3, h3, h3, h3, cos, sin, par, r0)


def _lru_gates(x, wr_ref, wi_ref, br, bi, lam):
    rs, is_ = [], []
    for h in range(N_HEADS):
        sl = slice(h * HEAD_DIM, (h + 1) * HEAD_DIM)
        xh = _b(x[:, sl])
        rs.append(_mm(xh, wr_ref[h]))
        is_.append(_mm(xh, wi_ref[h]))
    r = jax.nn.sigmoid(jnp.concatenate(rs, 1) + br)
    ig = jax.nn.sigmoid(jnp.concatenate(is_, 1) + bi)
    log_a = -LRU_C * r * jax.nn.softplus(-lam)
    a = jnp.exp(log_a)
    th = jnp.tanh(log_a)
    one_m_a2 = -2.0 * th / (1.0 - th)
    return a, jnp.sqrt(one_m_a2) * (ig * x)


def _lru_kernel(x_ref, z_ref, w_ref, b_ref, hist_ref, wr_ref, wi_ref, br_ref, bi_ref, lam_ref, h0_ref,
                y_ref, hfin_ref, scx, a_s, b_s, *, t, rb):
    for h in range(N_HEADS):
        sl = slice(h * HEAD_DIM, (h + 1) * HEAD_DIM)
        scx[h, pl.ds(0, 8), :] = hist_ref[:, sl]
        scx[h, pl.ds(8, t), :] = x_ref[:, sl]
    br, bi, lam = br_ref[...], bi_ref[...], lam_ref[...]

    def gates(i, carry):
        r0 = pl.multiple_of(i * rb, rb)
        xs = []
        for h in range(N_HEADS):
            sl = slice(h * HEAD_DIM, (h + 1) * HEAD_DIM)
            xh = b_ref[:, sl] + scx[h, pl.ds(r0 + 5, rb), :] * w_ref[0:1, sl]
            for j in range(1, CONV_W):
                xh = xh + scx[h, pl.ds(r0 + 5 + j, rb), :] * w_ref[j:j + 1, sl]
            xs.append(xh)
        x = jnp.concatenate(xs, 1)
        a, b = _lru_gates(x, wr_ref, wi_ref, br, bi, lam)
        a_s[pl.ds(r0, rb), :] = a
        b_s[pl.ds(r0, rb), :] = b
        return carry

    lax.fori_loop(0, t // rb, gates, 0)

    def scan(i, h):
        h = a_s[pl.ds(i, 1), :] * h + b_s[pl.ds(i, 1), :]
        b_s[pl.ds(i, 1), :] = h
        return h

    hfin_ref[...] = lax.fori_loop(0, t, scan, h0_ref[...], unroll=8)

    def gate_out(i, carry):
        rows = pl.ds(pl.multiple_of(i * rb, rb), rb)
        y_ref[rows, :] = b_s[rows, :] * _silu(z_ref[rows, :])
        return carry

    lax.fori_loop(0, t // rb, gate_out, 0)


def _lru(h3, hist8, conv_w, conv_b, w_r, w_i, b_r, b_i, lam, h0):
    bsz, t, _ = h3.shape
    rb = math.gcd(t, 256)
    xblk = (C_CONV + 3 * BRANCH_W) // BRANCH_W
    vec = pl.BlockSpec((1, BRANCH_W), lambda b: (0, 0))
    wsp = pl.BlockSpec((N_HEADS, HEAD_DIM, HEAD_DIM), lambda b: (0, 0, 0))
    return pl.pallas_call(
        functools.partial(_lru_kernel, t=t, rb=rb),
        grid=(bsz,),
        in_specs=[pl.BlockSpec((None, t, BRANCH_W), lambda b: (b, 0, xblk)),
                  pl.BlockSpec((None, t, BRANCH_W), lambda b: (b, 0, C_LRUZ // BRANCH_W)),
                  pl.BlockSpec((CONV_W, BRANCH_W), lambda b: (0, xblk)),
                  pl.BlockSpec((1, BRANCH_W), lambda b: (0, xblk)),
                  pl.BlockSpec((None, 8, BRANCH_W), lambda b: (b, 0, xblk)),
                  wsp, wsp, vec, vec, vec,
                  pl.BlockSpec((None, 1, BRANCH_W), lambda b: (b, 0, 0))],
        out_specs=[pl.BlockSpec((None, t, BRANCH_W), lambda b: (b, 0, 0)),
                   pl.BlockSpec((None, 1, BRANCH_W), lambda b: (b, 0, 0))],
        out_shape=[jax.ShapeDtypeStruct((bsz, t, BRANCH_W), F32), jax.ShapeDtypeStruct((bsz, 1, BRANCH_W), F32)],
        scratch_shapes=[pltpu.VMEM((N_HEADS, t + 8, HEAD_DIM), F32), pltpu.VMEM((t, BRANCH_W), F32),
                        pltpu.VMEM((t, BRANCH_W), F32)],
        compiler_params=_cp(1),
        name="rglru",
    )(h3, h3, conv_w, conv_b, hist8, w_r, w_i, b_r, b_i, lam, h0)


CHUNK_UNROLL = 4
BISECT_STEPS = 18


def _kth_threshold(count_ge, count_gt, next_above, lo, hi, k):
    c_hi = count_ge(hi)
    top = c_hi >= k
    lo = jnp.where(top, hi, lo)

    def halve(_, c):
        lo, hi = c
        mid = lo + (hi - lo) * 0.5
        up = count_ge(mid) >= k
        return jnp.where(up, mid, lo), jnp.where(up, hi, mid)

    lo, _ = lax.fori_loop(0, BISECT_STEPS, halve, (lo, hi))

    def more(c):
        return jnp.max(c[1] - k) > 0.5

    def walk(c):
        thr, c_gt = c
        nxt = jnp.where(c_gt > k, next_above(thr), thr)
        return nxt, count_gt(nxt)

    return lax.while_loop(more, walk, (lo, count_gt(lo)))


def _dsa_kernel(q_ref, z_ref, qi_ref, sm_ref, cos_ref, sin_ref, ci_ref, sia_ref, sib_ref,
                k_ref, v_ref, ki_ref, tri_ref, o_ref, sc_s, bias_s, *, tq, t, ksel, tile0):
    i = pl.program_id(1) + tile0
    ki16 = ki_ref[...]
    sm = sm_ref[...]
    ci, sia, sib = ci_ref[...], sia_ref[...], sib_ref[...]
    score = jnp.zeros((tq, t), F32)
    for h in range(IDX_HEADS):
        x = _rope_idx(qi_ref[:, h * LANE:(h + 1) * LANE], ci, sia, sib)
        s = _nt(_b(x), ki16)
        w = sm[:, 8 + h:9 + h] * IDX_SCALE
        score = score + jnp.maximum(s, 0.0) * w
    qpos = i * tq + lax.broadcasted_iota(jnp.int32, (tq, 1), 0)
    kpos = lax.broadcasted_iota(jnp.int32, (1, t), 1)
    valid = kpos <= qpos
    sc_s[...] = jnp.where(valid, score, -jnp.inf)
    k_row = jnp.minimum(float(ksel), (qpos + 1).astype(F32))

    def count_ge(th):
        return jnp.sum(jnp.where(sc_s[...] >= th, 1.0, 0.0), -1, keepdims=True)

    def count_gt(th):
        return jnp.sum(jnp.where(sc_s[...] > th, 1.0, 0.0), -1, keepdims=True)

    def next_above(th):
        s = sc_s[...]
        return jnp.min(jnp.where(s > th, s, jnp.inf), -1, keepdims=True)

    lo = jnp.min(jnp.where(valid, score, jnp.inf), -1, keepdims=True)
    hi = jnp.max(sc_s[...], -1, keepdims=True)
    thr, n_gt = _kth_threshold(count_ge, count_gt, next_above, lo, hi, k_row)
    need = k_row - n_gt
    run = jnp.zeros((tq, 1), F32)
    tri = tri_ref[...]
    for cb in range(t // LANE):
        sl = slice(cb * LANE, (cb + 1) * LANE)
        sc = sc_s[:, sl]
        eq = sc == thr
        pre = _mm(jnp.where(eq, 1.0, 0.0).astype(BF16), tri) + run
        bias_s[:, sl] = jnp.where(sc > thr, 0.0, jnp.where(eq, jnp.where(pre <= need, 0.0, NEG), NEG))
        run = pre[:, LANE - 1:LANE]
    bias = bias_s[...]
    cos, sin = cos_ref[...], sin_ref[...]
    for h in range(N_HEADS):
        sl = slice(h * HEAD_DIM, (h + 1) * HEAD_DIM)
        q = _rope128(q_ref[:, sl], cos, sin)
        lg = _nt(_b(q), k_ref[:, sl]) * SCALE + bias
        m = jnp.max(lg, -1, keepdims=True)
        p = jnp.exp(lg - m)
        l = jnp.sum(p, -1, keepdims=True)
        o = _mm(_b(p), v_ref[:, sl]) / l
        o_ref[:, sl] = o * _silu(z_ref[:, sl])


def _tri128():
    return jnp.asarray(np.triu(np.ones((LANE, LANE), np.float32)), BF16)


DSA_KEY_CLASSES = 4


def _dsa_class(h3, k16, v16, ki16, tabs, tq, tile0, n_tiles, t_keys, ksel):
    bsz = h3.shape[0]
    cos, sin, ci, sia, sib = tabs
    qtab = pl.BlockSpec((tq, LANE), lambda b, i: (i + tile0, 0))
    keys = lambda w: pl.BlockSpec((None, t_keys, w), lambda b, i: (b, 0, 0))
    return pl.pallas_call(
        functools.partial(_dsa_kernel, tq=tq, t=t_keys, ksel=ksel, tile0=tile0),
        grid=(bsz, n_tiles),
        in_specs=[pl.BlockSpec((None, tq, 512), lambda b, i: (b, i + tile0, C_DSA // 512)),
                  pl.BlockSpec((None, tq, 512), lambda b, i: (b, i + tile0, (C_DSA + 1536) // 512)),
                  pl.BlockSpec((None, tq, 1024), lambda b, i: (b, i + tile0, C_QI // 1024)),
                  pl.BlockSpec((None, tq, LANE), lambda b, i: (b, i + tile0, C_SMALL // LANE)),
                  qtab, qtab, qtab, qtab, qtab,
                  keys(512), keys(512), keys(LANE),
                  pl.BlockSpec((LANE, LANE), lambda b, i: (0, 0))],
        out_specs=pl.BlockSpec((None, tq, 512), lambda b, i: (b, i, 0)),
        out_shape=jax.ShapeDtypeStruct((bsz, n_tiles * tq, BRANCH_W), F32),
        scratch_shapes=[pltpu.VMEM((tq, t_keys), F32), pltpu.VMEM((tq, t_keys), F32)],
        compiler_params=_cp(2),
        name="dsa_prompt",
    )(h3, h3, h3, h3, cos, sin, ci, sia, sib, k16, v16, ki16, _tri128())


def _dsa(h3, k16, v16, ki16, tabs, tq):
    _, t, _ = h3.shape
    ksel = min(DSA_TOPK, t // 4)
    n_tiles = t // tq
    n_cls = math.gcd(n_tiles, DSA_KEY_CLASSES)
    per = n_tiles // n_cls
    outs = [_dsa_class(h3, k16, v16, ki16, tabs, tq, c * per, per, (c + 1) * per * tq, ksel) for c in range(n_cls)]
    return jnp.concatenate(outs, 1)


def _mem_kernel(q_ref, z_ref, mk_ref, mv_ref, o_ref):
    for h in range(N_HEADS):
        sl = slice(h * HEAD_DIM, (h + 1) * HEAD_DIM)
        lg = _nt(_b(q_ref[:, sl]), _b(mk_ref[:, sl])) * SCALE
        m = jnp.max(lg, -1, keepdims=True)
        p = jnp.exp(lg - m)
        l = jnp.sum(p, -1, keepdims=True)
        o = _mm(_b(p), _b(mv_ref[:, sl])) / l
        o_ref[:, sl] = o * _silu(z_ref[:, sl])


def _mem(h3, mkv3, tq):
    bsz, t, _ = h3.shape
    nm = mkv3.shape[1]
    return pl.pallas_call(
        _mem_kernel,
        grid=(bsz, t // tq),
        in_specs=[pl.BlockSpec((None, tq, 512), lambda b, i: (b, i, C_MEM // 512)),
                  pl.BlockSpec((None, tq, 512), lambda b, i: (b, i, C_MEM // 512 + 1)),
                  pl.BlockSpec((None, nm, 512), lambda b, i: (b, 0, 0)),
                  pl.BlockSpec((None, nm, 512), lambda b, i: (b, 0, 1))],
        out_specs=pl.BlockSpec((None, tq, 512), lambda b, i: (b, i, 0)),
        out_shape=jax.ShapeDtypeStruct((bsz, t, BRANCH_W), F32),
        compiler_params=_cp(2),
        name="mem_attn",
    )(h3, h3, mkv3, mkv3)


def _merge_kernel(ya_ref, yb_ref, yc_ref, yd_ref, ye_ref, g_ref, w_ref, o_ref):
    acc = None
    for nbr, y_ref in enumerate((ya_ref, yb_ref, yc_ref, yd_ref, ye_ref)):
        p = _mm(_b(y_ref[...]), w_ref[nbr])
        g = jax.nn.sigmoid(g_ref[:, nbr * D_MODEL:(nbr + 1) * D_MODEL].astype(F32))
        acc = g * p if acc is None else acc + g * p
    o_ref[...] = _b(acc)


def _merge(ys, g, w_br, tm):
    m = g.shape[0]
    ysp = pl.BlockSpec((tm, BRANCH_W), lambda i: (i, 0))
    return pl.pallas_call(
        _merge_kernel,
        grid=(m // tm,),
        in_specs=[ysp] * N_BRANCH + [pl.BlockSpec((tm, N_GATE), lambda i: (i, 0)),
                                     pl.BlockSpec((N_BRANCH, BRANCH_W, D_MODEL), lambda i: (0, 0, 0))],
        out_specs=pl.BlockSpec((tm, D_MODEL), lambda i: (i, 0)),
        out_shape=jax.ShapeDtypeStruct((m, D_MODEL), BF16),
        compiler_params=_cp(1),
        name="gated_merge",
    )(*ys, g, w_br)


def _outnorm_kernel(m_ref, x_ref, w_ref, g_ref, b_ref, y_ref, y16_ref):
    v = DEEPNORM_ALPHA * x_ref[...] + _mm(m_ref[...], w_ref[...])
    mu = jnp.mean(v, -1, keepdims=True)
    d = v - mu
    var = jnp.mean(d * d, -1, keepdims=True)
    y = d * lax.rsqrt(var + LN_EPS) * g_ref[...] + b_ref[...]
    y_ref[...] = y
    y16_ref[...] = _b(y)


def _outnorm(merged, x, w_out, ln_g, ln_b, tm):
    m = x.shape[0]
    row = pl.BlockSpec((tm, D_MODEL), lambda i: (i, 0))
    vec = pl.BlockSpec((1, D_MODEL), lambda i: (0, 0))
    return pl.pallas_call(
        _outnorm_kernel,
        grid=(m // tm,),
        in_specs=[row, row, pl.BlockSpec((D_MODEL, D_MODEL), lambda i: (0, 0)), vec, vec],
        out_specs=[row, row],
        out_shape=[jax.ShapeDtypeStruct((m, D_MODEL), F32), jax.ShapeDtypeStruct((m, D_MODEL), BF16)],
        compiler_params=_cp(1),
        name="out_norm",
    )(merged, x, w_out, ln_g, ln_b)


def _layer_params(l, w_in, conv_w, conv_b, gdn_a_log, gdn_dt_bias, gdn_norm_w, lru_lambda, lru_w_r, lru_b_r,
                  lru_w_i, lru_b_i, w_mem_kv, w_branch, w_out, ln_g, ln_b):
    w_main, w_gate = _prep_w_in(w_in, l)
    rows = jnp.stack([jnp.broadcast_to(gdn_a_log[l][:, None], (N_HEADS, LANE)),
                      jnp.broadcast_to(gdn_dt_bias[l][:, None], (N_HEADS, LANE)),
                      jnp.broadcast_to(gdn_norm_w[l][None, :], (N_HEADS, LANE))], 1)
    gdn_par = jnp.concatenate([rows, jnp.zeros((N_HEADS, 5, LANE), F32)], 1)
    return dict(
        w_main=w_main, w_gate=w_gate, conv_w=conv_w[l], conv_b=conv_b[l][None, :], gdn_par=gdn_par,
        ret_par=_ret_par(), lru_w_r=_b(lru_w_r[l]), lru_w_i=_b(lru_w_i[l]),
        lru_b_r=lru_b_r[l][None, :], lru_b_i=lru_b_i[l][None, :], lam=lru_lambda[l][None, :],
        w_mem_kv=_b(w_mem_kv[l]), w_branch=_b(w_branch[l]), w_out=_b(w_out[l]),
        ln_g=ln_g[l][None, :], ln_b=ln_b[l][None, :])


def _hist8(buf):
    return jnp.concatenate([jnp.zeros((buf.shape[0], 8 - (CONV_W - 1), buf.shape[2]), buf.dtype), buf], 1)


def _prompt_layer(x, x16, mem16, p, tabs):
    bsz, t, _ = x.shape
    m = bsz * t
    tm = math.gcd(m, 1024)
    h = _matmul_nt(x16, p["w_main"], F32, tm, 512)
    g = _matmul_nt(x16, p["w_gate"], BF16, tm, 1024)
    mkv = _matmul(mem16, p["w_mem_kv"], F32, math.gcd(mem16.shape[0], 512), 512)
    h3 = h.reshape(bsz, t, N_MAIN)
    mkv3 = mkv.reshape(bsz, -1, 2 * BRANCH_W)
    zero_hist = jnp.zeros((bsz, 8, CONV_DIM), F32)
    zero_state = jnp.zeros((bsz, N_HEADS, HEAD_DIM, HEAD_DIM), F32)
    cos, sin = tabs[0], tabs[1]
    y_a, s_gdn = _gdn(h3, zero_hist, p["conv_w"], p["conv_b"], p["gdn_par"], zero_state)
    kb, k16, v16, kif, ki16 = _rope_k(h3, tabs, math.gcd(t, 512))
    y_b = _dsa(h3, k16, v16, ki16, tabs, math.gcd(t, 256))
    y_c, s_ret = _ret(h3, cos, sin, p["ret_par"], zero_state)
    y_d, s_lru = _lru(h3, zero_hist, p["conv_w"], p["conv_b"], p["lru_w_r"], p["lru_w_i"], p["lru_b_r"],
                      p["lru_b_i"], p["lam"], jnp.zeros((bsz, 1, BRANCH_W), F32))
    y_e = _mem(h3, mkv3, math.gcd(t, 512))
    ys = [a.reshape(m, BRANCH_W) for a in (y_a, y_b, y_c, y_d, y_e)]
    merged = _merge(ys, g, p["w_branch"], math.gcd(m, 256))
    y, y16 = _outnorm(merged, x.reshape(m, D_MODEL), p["w_out"], p["ln_g"], p["ln_b"], math.gcd(m, 512))
    new_conv = h3[:, t - (CONV_W - 1):, C_CONV:C_CONV + CONV_DIM]
    v_rows = h3[:, :, C_DSA + 1024:C_DSA + 1536].reshape(bsz, t, N_HEADS, HEAD_DIM)
    mk = mkv3[:, :, :BRANCH_W].reshape(bsz, -1, N_HEADS, HEAD_DIM)
    mv = mkv3[:, :, BRANCH_W:].reshape(bsz, -1, N_HEADS, HEAD_DIM)
    return (y.reshape(bsz, t, D_MODEL), y16,
            (new_conv, s_gdn, s_ret, s_lru.reshape(bsz, BRANCH_W)),
            (kb.reshape(bsz, t, N_HEADS, HEAD_DIM), v_rows, kif[:, :, :IDX_DIM]), mk, mv)


def _dec_kernel(hs_ref, hist_ref, cw_ref, cb_ref, gpar_ref, rpar_ref, sg_ref, sr_ref, hl_ref,
                wr_ref, wi_ref, br_ref, bi_ref, lam_ref, mk_ref, mv_ref,
                cos_ref, sin_ref, ci_ref, sia_ref, sib_ref,
                ya_ref, yc_ref, yd_ref, ye_ref, sgo_ref, sro_ref, hlo_ref, qb_ref, kb_ref, qi_ref, ki_ref, ss_ref,
                wk_s, vb_s, qd_s, qkg_s, eg_s, kg_s, qx_s, qkr_s, kr_s, vr_s, mq_s, oa_s, oc_s, oe_s, *, nb):
    hd = HEAD_DIM
    hsl = lambda h: slice(h * hd, (h + 1) * hd)
    x_in = hs_ref[:, C_CONV:C_CONV + CONV_DIM]
    conv = cb_ref[...] + x_in * cw_ref[CONV_W - 1:CONV_W, :]
    for j in range(CONV_W - 1):
        conv = conv + hist_ref[j] * cw_ref[j:j + 1, :]
    sm = hs_ref[:, C_SMALL:C_SMALL + LANE]
    cos, sin = cos_ref[...], sin_ref[...]
    ci, sia, sib = ci_ref[...], sia_ref[...], sib_ref[...]
    ones = jnp.ones((1, hd), F32)
    for h in range(N_HEADS):
        sl = hsl(h)
        q = _silu(conv[:, h * hd:(h + 1) * hd])
        k = _silu(conv[:, BRANCH_W + h * hd:BRANCH_W + (h + 1) * hd])
        v = _silu(conv[:, 2 * BRANCH_W + h * hd:2 * BRANCH_W + (h + 1) * hd])
        q = q * lax.rsqrt(jnp.sum(q * q, -1, keepdims=True) + NORM_EPS) * SCALE
        k = k * lax.rsqrt(jnp.sum(k * k, -1, keepdims=True) + NORM_EPS)
        beta = jax.nn.sigmoid(sm[:, N_HEADS + h:N_HEADS + h + 1])
        g = -jnp.exp(gpar_ref[h, 0:1, 0:1]) * jax.nn.softplus(sm[:, h:h + 1] + gpar_ref[h, 1:2, 0:1])
        eg = jnp.exp(g)
        wk_s[h] = k * (beta * eg)
        vb_s[h] = v * beta
        qd_s[h] = q * eg
        qkg_s[h] = jnp.sum(q * k, -1, keepdims=True) * ones
        eg_s[h] = eg * ones
        kg_s[h] = k
        q = _rope128(hs_ref[:, C_RET + h * hd:C_RET + (h + 1) * hd], cos, sin)
        k = _rope128(hs_ref[:, C_RET + BRANCH_W + h * hd:C_RET + BRANCH_W + (h + 1) * hd], cos, sin) * SCALE
        qx_s[h] = q * rpar_ref[h, 0:1, 0:1]
        qkr_s[h] = jnp.sum(q * k, -1, keepdims=True) * ones
        kr_s[h] = k
        vr_s[h] = hs_ref[:, C_RET + 2 * BRANCH_W + h * hd:C_RET + 2 * BRANCH_W + (h + 1) * hd]
        mq_s[h] = hs_ref[:, C_MEM + h * hd:C_MEM + (h + 1) * hd]
        qb_ref[:, sl] = _rope128(hs_ref[:, C_DSA + h * hd:C_DSA + (h + 1) * hd], cos, sin)
        kb_ref[:, sl] = _rope128(hs_ref[:, C_DSA + BRANCH_W + h * hd:C_DSA + BRANCH_W + (h + 1) * hd], cos, sin)
    ki = _rope_idx(hs_ref[:, C_KI:C_KI + LANE], ci, sia, sib)
    ki_ref[...] = ki
    sself = jnp.zeros((nb, 1), F32)
    for h in range(IDX_HEADS):
        qi = _rope_idx(hs_ref[:, C_QI + h * LANE:C_QI + (h + 1) * LANE], ci, sia, sib)
        qi_ref[:, h * LANE:(h + 1) * LANE] = qi
        s = jnp.sum(qi * ki, -1, keepdims=True)
        sself = sself + jnp.maximum(s, 0.0) * (sm[:, 8 + h:9 + h] * IDX_SCALE)
    ss_ref[...] = sself * jnp.ones((1, LANE), F32)

    ii = lax.broadcasted_iota(jnp.int32, (hd, hd), 0)
    jj = lax.broadcasted_iota(jnp.int32, (hd, hd), 1)
    eye = jnp.where(ii == jj, 1.0, 0.0).astype(F32)

    def col_bcast(row):
        return _nt(eye, jnp.broadcast_to(row, (hd, hd)), HI)

    def per_batch(bb, carry):
        r = pl.ds(bb, 1)
        for h in range(N_HEADS):
            sl = hsl(h)
            s = sg_ref[bb, h]
            v_new = vb_s[h, r, :] - _mm(wk_s[h, r, :], s, HI)
            oa_s[h, r, :] = _mm(qd_s[h, r, :], s, HI) + qkg_s[h, r, :] * v_new
            sgo_ref[bb, h] = s * eg_s[h, r, :] + col_bcast(kg_s[h, r, :]) * v_new
            rs = sr_ref[bb, h]
            v = vr_s[h, r, :]
            oc_s[h, r, :] = _mm(qx_s[h, r, :], rs, HI) + qkr_s[h, r, :] * v
            sro_ref[bb, h] = rs * rpar_ref[h, 0:1, :] + col_bcast(kr_s[h, r, :]) * v
            lg = _nt(_b(mq_s[h, r, :]), _b(mk_ref[bb, :, sl])) * SCALE
            m = jnp.max(lg, -1, keepdims=True)
            p = jnp.exp(lg - m)
            oe_s[h, r, :] = _mm(_b(p), _b(mv_ref[bb, :, sl])) / jnp.sum(p, -1, keepdims=True)
        return carry

    lax.fori_loop(0, nb, per_batch, 0)

    for h in range(N_HEADS):
        sl = hsl(h)
        o = oa_s[h]
        o = o * lax.rsqrt(jnp.mean(o * o, -1, keepdims=True) + NORM_EPS) * gpar_ref[h, 2:3, :]
        ya_ref[:, sl] = o * _silu(hs_ref[:, C_GDNZ + h * hd:C_GDNZ + (h + 1) * hd])
        o = oc_s[h]
        d = o - jnp.mean(o, -1, keepdims=True)
        var = jnp.mean(d * d, -1, keepdims=True)
        yc_ref[:, sl] = d * lax.rsqrt(var + LN_EPS) * _silu(hs_ref[:, C_RET + 3 * BRANCH_W + h * hd:C_RET + 3 * BRANCH_W + (h + 1) * hd])
        ye_ref[:, sl] = oe_s[h] * _silu(hs_ref[:, C_MEM + BRANCH_W + h * hd:C_MEM + BRANCH_W + (h + 1) * hd])
    a, b = _lru_gates(conv[:, 3 * BRANCH_W:], wr_ref, wi_ref, br_ref[...], bi_ref[...], lam_ref[...])
    h_new = a * hl_ref[...] + b
    hlo_ref[...] = h_new
    yd_ref[...] = h_new * _silu(hs_ref[:, C_LRUZ:C_LRUZ + BRANCH_W])


def _decode(hs, hist_t, p, sg, sr, hl, mk3, mv3, tabs):
    nb = hs.shape[0]
    bw = jax.ShapeDtypeStruct((nb, BRANCH_W), F32)
    st = jax.ShapeDtypeStruct((nb, N_HEADS, HEAD_DIM, HEAD_DIM), F32)
    out_shape = [bw, bw, bw, bw, st, st, bw, bw, bw,
                 jax.ShapeDtypeStruct((nb, IDX_HEADS * LANE), F32), jax.ShapeDtypeStruct((nb, LANE), F32),
                 jax.ShapeDtypeStruct((nb, LANE), F32)]
    return pl.pallas_call(
        functools.partial(_dec_kernel, nb=nb),
        out_shape=out_shape,
        scratch_shapes=[pltpu.VMEM((N_HEADS, nb, HEAD_DIM), F32)] * 14,
        compiler_params=pltpu.CompilerParams(vmem_limit_bytes=VMEM_LIMIT),
        name="decode_step",
    )(hs, hist_t, p["conv_w"], p["conv_b"], p["gdn_par"], p["ret_par"], sg, sr, hl,
      p["lru_w_r"], p["lru_w_i"], p["lru_b_r"], p["lru_b_i"], p["lam"], mk3, mv3, *tabs)


PAGES_PER_STEP = 16


IDX_PAGES_PER_STEP = 32


def _didx_kernel(pt_ref, qi_ref, w_ref, *rest):
    pages, o_ref = rest[:-1], rest[-1]
    q16 = _b(qi_ref[:, :IDX_DIM])
    w = w_ref[...]
    for j, pg in enumerate(pages):
        s = _mm(q16, _b(pg[...]))
        o_ref[j:j + 1, :] = jnp.sum(jnp.maximum(s, 0.0) * w, 0, keepdims=True)


def _didx(page_table, qi3, wcol, pool_ikt, l):
    nb, n_pages = page_table.shape
    pps = math.gcd(n_pages, IDX_PAGES_PER_STEP)
    ng = n_pages // pps
    page = lambda j: pl.BlockSpec((None, None, IDX_DIM, PAGE_SIZE),
                                  lambda b, g, pt, j=j: (l, pt[b, g * pps + j], 0, 0))
    return pl.pallas_call(
        _didx_kernel,
        grid_spec=pltpu.PrefetchScalarGridSpec(
            num_scalar_prefetch=1, grid=(nb, ng),
            in_specs=[pl.BlockSpec((None, IDX_HEADS, LANE), lambda b, g, pt: (b, 0, 0)),
                      pl.BlockSpec((None, IDX_HEADS, 1), lambda b, g, pt: (b, 0, 0))]
            + [page(j) for j in range(pps)],
            out_specs=pl.BlockSpec((None, None, pps, PAGE_SIZE), lambda b, g, pt: (b, g, 0, 0))),
        out_shape=jax.ShapeDtypeStruct((nb, ng, pps, PAGE_SIZE), F32),
        compiler_params=_cp(2),
        name="paged_indexer",
    )(page_table, qi3, wcol, *([pool_ikt] * pps))


NEG_SEL = -2e30
M_INIT = -1e30


def _dsel_kernel(sc_ref, ss_ref, tri_ref, bias_ref, sbias_ref, *, n_pages, ksel):
    key = sc_ref[...]
    kself = ss_ref[:, 0:1]

    def reduce2(x, op):
        return op(op(x, 1, keepdims=True), 0, keepdims=True)

    def count_ge(th):
        return reduce2(jnp.where(key >= th, 1.0, 0.0), jnp.sum) + jnp.where(kself >= th, 1.0, 0.0)

    def count_gt(th):
        return reduce2(jnp.where(key > th, 1.0, 0.0), jnp.sum) + jnp.where(kself > th, 1.0, 0.0)

    def next_above(th):
        return jnp.minimum(reduce2(jnp.where(key > th, key, jnp.inf), jnp.min), jnp.where(kself > th, kself, jnp.inf))

    lo = jnp.minimum(reduce2(key, jnp.min), kself)
    hi = jnp.maximum(reduce2(key, jnp.max), kself)
    thr, n_gt = _kth_threshold(count_ge, count_gt, next_above, lo, hi, float(ksel))
    eq = key == thr
    need = ksel - n_gt
    pre_in = _mm(jnp.where(eq, 1.0, 0.0).astype(BF16), tri_ref[...])
    tot = pre_in[:, PAGE_SIZE - 1:PAGE_SIZE]
    ii = lax.broadcasted_iota(jnp.int32, (n_pages, n_pages), 0)
    jj = lax.broadcasted_iota(jnp.int32, (n_pages, n_pages), 1)
    below = jnp.where(ii > jj, 1.0, 0.0).astype(BF16)
    pre = pre_in + _mm(below, _b(jnp.broadcast_to(tot, (n_pages, PAGE_SIZE))))
    bias_ref[...] = jnp.where(key > thr, 0.0, jnp.where(eq, jnp.where(pre <= need, 0.0, NEG_SEL), NEG_SEL))
    n_eq = pre[n_pages - 1:n_pages, PAGE_SIZE - 1:PAGE_SIZE]
    sb = jnp.where(kself > thr, 0.0, jnp.where(kself == thr, jnp.where(n_eq + 1.0 <= need, 0.0, NEG_SEL), NEG_SEL))
    sbias_ref[...] = sb * jnp.ones((1, LANE), F32)


def _dsel(scores3, sself3):
    nb, n_pages, _ = scores3.shape
    ksel = min(DSA_TOPK, (n_pages * PAGE_SIZE + 1) // 4)
    return pl.pallas_call(
        functools.partial(_dsel_kernel, n_pages=n_pages, ksel=ksel),
        grid=(nb,),
        in_specs=[pl.BlockSpec((None, n_pages, PAGE_SIZE), lambda b: (b, 0, 0)),
                  pl.BlockSpec((None, 1, LANE), lambda b: (b, 0, 0)),
                  pl.BlockSpec((LANE, LANE), lambda b: (0, 0))],
        out_specs=[pl.BlockSpec((None, n_pages, PAGE_SIZE), lambda b: (b, 0, 0)),
                   pl.BlockSpec((None, 1, LANE), lambda b: (b, 0, 0))],
        out_shape=[jax.ShapeDtypeStruct((nb, n_pages, PAGE_SIZE), F32), jax.ShapeDtypeStruct((nb, 1, LANE), F32)],
        compiler_params=_cp(1),
        name="sample_select",
    )(scores3, sself3, _tri128())


def _datt_kernel(pt_ref, q_ref, bias_ref, sbias_ref, kself_ref, vself_ref, z_ref, *rest, ng, pps):
    kpages = rest[:pps]
    vpages = rest[pps:2 * pps]
    o_ref, m_s, l_s, acc_s = rest[2 * pps:]
    g = pl.program_id(1)

    @pl.when(g == 0)
    def _():
        m_s[...] = jnp.full(m_s.shape, M_INIT, F32)
        l_s[...] = jnp.zeros(l_s.shape, F32)
        acc_s[...] = jnp.zeros(acc_s.shape, F32)

    rows = PAGE_SIZE * N_HEADS
    sub = lax.broadcasted_iota(jnp.int32, (8, rows), 0)
    own = lax.broadcasted_iota(jnp.int32, (8, rows), 1) % N_HEADS == sub
    q8 = q_ref[...]
    q16 = _b(q8)

    def update(lg, pv):
        m_prev = m_s[...]
        m_new = jnp.maximum(m_prev, jnp.max(lg, -1, keepdims=True))
        alpha = jnp.exp(m_prev - m_new)
        p = jnp.exp(lg - m_new)
        l_s[...] = l_s[...] * alpha + jnp.sum(p, -1, keepdims=True)
        acc_s[...] = acc_s[...] * alpha + pv(p)
        m_s[...] = m_new

    lg = jnp.concatenate(
        [jnp.where(own, _nt(q16, _b(kpages[j][...])) * SCALE + bias_ref[j:j + 1, :], NEG_SEL)
         for j in range(pps)], 1)

    def pv_pages(p):
        acc = _mm(_b(p[:, 0:rows]), _b(vpages[0][...]))
        for j in range(1, pps):
            acc = acc + _mm(_b(p[:, j * rows:(j + 1) * rows]), _b(vpages[j][...]))
        return acc

    update(lg, pv_pages)

    @pl.when(g == ng - 1)
    def _():
        lgs = jnp.sum(q8 * kself_ref[...], -1, keepdims=True) * SCALE + sbias_ref[:, 0:1]
        update(lgs, lambda p: p * vself_ref[...])
        l = l_s[...]
        o_ref[...] = acc_s[...] / jnp.where(l > 0.0, l, 1.0) * _silu(z_ref[...])


def _datt(page_table, q8, bias4, sbias3, kself8, vself8, z8, pool_k, pool_v, l):
    nb, n_pages = page_table.shape
    pps = math.gcd(n_pages, PAGES_PER_STEP)
    ng = n_pages // pps
    rows = PAGE_SIZE * N_HEADS
    head = pl.BlockSpec((None, 8, HEAD_DIM), lambda b, g, pt: (b, 0, 0))
    page = lambda j: pl.BlockSpec((None, None, rows, HEAD_DIM),
                                  lambda b, g, pt, j=j: (l, pt[b, g * pps + j], 0, 0))
    return pl.pallas_call(
        functools.partial(_datt_kernel, ng=ng, pps=pps),
        grid_spec=pltpu.PrefetchScalarGridSpec(
            num_scalar_prefetch=1, grid=(nb, ng),
            in_specs=[head, pl.BlockSpec((None, pps, rows), lambda b, g, pt: (b, g, 0)),
                      pl.BlockSpec((None, 1, LANE), lambda b, g, pt: (b, 0, 0)), head, head, head]
            + [page(j) for j in range(pps)] * 2,
            out_specs=head,
            scratch_shapes=[pltpu.VMEM((8, 1), F32), pltpu.VMEM((8, 1), F32), pltpu.VMEM((8, HEAD_DIM), F32)]),
        out_shape=jax.ShapeDtypeStruct((nb, 8, HEAD_DIM), F32),
        compiler_params=_cp(2),
        name="paged_attention",
    )(page_table, q8, bias4, sbias3, kself8, vself8, z8, *([pool_k] * pps), *([pool_v] * pps))


def _sample_layer(x, x16, p, tabs, conv_buf, sg, sr, hl, mem_k, mem_v, pool_k, pool_v, pool_ik, page_table, l):
    nb = x.shape[0]
    hs = _matmul_nt(x16, p["w_main"], F32, nb, 512)
    gs = _matmul_nt(x16, p["w_gate"], BF16, nb, 1024)
    mk3 = mem_k.reshape(nb, -1, BRANCH_W)
    mv3 = mem_v.reshape(nb, -1, BRANCH_W)
    (y_a, y_c, y_d, y_e, sg_new, sr_new, hl_new, qb, kb, qi, ki, sself) = _decode(
        hs, conv_buf.transpose(1, 0, 2), p, sg, sr, hl, mk3, mv3, tabs)
    wcol = (hs[:, C_SMALL + 8:C_SMALL + 16] * IDX_SCALE)[:, :, None]
    scores = _didx(page_table, qi.reshape(nb, IDX_HEADS, LANE), wcol, jnp.swapaxes(pool_ik, 2, 3), l)
    n_pages = page_table.shape[1]
    bias3, sbias3 = _dsel(scores.reshape(nb, n_pages, PAGE_SIZE), sself[:, None, :])
    v_self = hs[:, C_DSA + 1024:C_DSA + 1536]

    def heads8(a):
        a = a.reshape(nb, N_HEADS, HEAD_DIM)
        return jnp.concatenate([a, jnp.zeros((nb, 8 - N_HEADS, HEAD_DIM), a.dtype)], 1)

    pages = lambda pool: pool.reshape(*pool.shape[:2], PAGE_SIZE * N_HEADS, HEAD_DIM)
    y_b = _datt(page_table, heads8(qb), jnp.repeat(bias3, N_HEADS, axis=-1), sbias3, heads8(kb), heads8(v_self),
                heads8(hs[:, C_DSA + 1536:C_DSA + 2048]), pages(pool_k), pages(pool_v), l)
    merged = _merge([y_a, y_b[:, :N_HEADS].reshape(nb, BRANCH_W), y_c, y_d, y_e], gs, p["w_branch"], nb)
    y, y16 = _outnorm(merged, x.reshape(nb, D_MODEL), p["w_out"], p["ln_g"], p["ln_b"], nb)
    new_conv = jnp.concatenate([conv_buf[:, 1:], hs[:, None, C_CONV:C_CONV + CONV_DIM]], 1)
    return (y.reshape(nb, 1, D_MODEL), y16, (new_conv, sg_new, sr_new, hl_new),
            (kb.reshape(nb, 1, N_HEADS, HEAD_DIM), v_self.reshape(nb, 1, N_HEADS, HEAD_DIM), ki[:, None, :IDX_DIM]))


def kernel(x_prompt, x_sample, cache_conv, cache_dsa_k, cache_dsa_v, cache_dsa_idx_k, cache_mem_k, cache_mem_v,
           state_gdn, state_ret, state_lru, page_table, mem_prompt, w_in, conv_w, conv_b, gdn_a_log, gdn_dt_bias,
           gdn_norm_w, lru_lambda, lru_w_r, lru_b_r, lru_w_i, lru_b_i, w_mem_kv, w_branch, w_out, ln_g, ln_b):
    bsz, t, _ = x_prompt.shape
    nb = x_sample.shape[0]
    depth = w_in.shape[0]
    past = page_table.shape[1] * PAGE_SIZE
    tabs_p = _rope_tables(np.arange(t))
    tabs_s = _rope_tables(np.asarray([past]))
    mem16 = _b(mem_prompt.reshape(-1, D_MODEL))
    xp, xs = x_prompt, x_sample
    xp16, xs16 = _b(xp.reshape(-1, D_MODEL)), _b(xs.reshape(-1, D_MODEL))
    acc_p, acc_s = [], []
    for l in range(depth):
        p = _layer_params(l, w_in, conv_w, conv_b, gdn_a_log, gdn_dt_bias, gdn_norm_w, lru_lambda, lru_w_r, lru_b_r,
                          lru_w_i, lru_b_i, w_mem_kv, w_branch, w_out, ln_g, ln_b)
        xp, xp16, st, rows, mk, mv = _prompt_layer(xp, xp16, mem16, p, tabs_p)
        acc_p.append(st + rows + (mk, mv))
        xs, xs16, st, rows = _sample_layer(xs, xs16, p, tabs_s, cache_conv[l], state_gdn[l], state_ret[l], state_lru[l],
                                           cache_mem_k[l], cache_mem_v[l], cache_dsa_k, cache_dsa_v,
                                           cache_dsa_idx_k, page_table, l)
        acc_s.append(st + rows)
    stack = lambda acc, i: jnp.stack([a[i] for a in acc])
    return (xp, xs) + tuple(stack(acc_p, i) for i in range(9)) + tuple(stack(acc_s, i) for i in range(7))
```

```python
import functools
import math

import numpy as np
import jax
import jax.numpy as jnp
from jax import lax
from jax.experimental import pallas as pl
from jax.experimental.pallas import tpu as pltpu

F32 = jnp.float32
BF16 = jnp.bfloat16
HI = lax.Precision.HIGHEST

D_MODEL = 2048
HEAD_DIM = 128
BRANCH_W = 512
N_HEADS = 4
N_BRANCH = 5
CONV_W = 4
CONV_DIM = 2048
GDN_CHUNK = 64
GDN_GROUP = 256
RET_CHUNK = 256
DSA_TOPK = 256
IDX_HEADS = 8
IDX_DIM = 64
LRU_C = 8.0
ROPE_THETA = 10000.0
PAGE_SIZE = 128
LN_EPS = 1e-5
NORM_EPS = 1e-6
DEPTH = 2
DEEPNORM_ALPHA = (2 * DEPTH) ** 0.25
IN_SPLITS = (CONV_DIM, BRANCH_W, N_HEADS, N_HEADS,
             BRANCH_W, BRANCH_W, BRANCH_W, BRANCH_W, IDX_HEADS * IDX_DIM, IDX_DIM, IDX_HEADS,
             BRANCH_W, BRANCH_W, BRANCH_W, BRANCH_W,
             BRANCH_W,
             BRANCH_W, BRANCH_W,
             N_BRANCH * D_MODEL)

LANE = 128
C_CONV = 0
C_GDNZ = 2048
C_DSA = 2560
C_RET = 4608
C_LRUZ = 6656
C_MEM = 7168
C_QI = 8192
C_KI = 9216
C_SMALL = 9344
N_MAIN = 9728
N_GATE = N_BRANCH * D_MODEL
SCALE = HEAD_DIM ** -0.5
IDX_SCALE = (IDX_HEADS * IDX_DIM) ** -0.5
NEG = -1e30
VMEM_LIMIT = 56 * 1024 * 1024


def _cp(n_grid):
    return pltpu.CompilerParams(dimension_semantics=("arbitrary",) * n_grid, vmem_limit_bytes=VMEM_LIMIT)


def _b(x):
    return x.astype(BF16)


def _mm(a, b, precision=None):
    return jnp.dot(a, b, precision=precision, preferred_element_type=F32)


def _nt(a, b, precision=None):
    return lax.dot_general(a, b, (((1,), (1,)), ((), ())), precision=precision, preferred_element_type=F32)


def _tn(a, b, precision=None):
    return lax.dot_general(a, b, (((0,), (0,)), ((), ())), precision=precision, preferred_element_type=F32)


def _split3(x):
    hi = _b(x)
    r = x - hi.astype(F32)
    mid = _b(r)
    return hi, mid, _b(r - mid.astype(F32))


def _dot_pieces(dot, sel16, pieces):
    hi, mid, lo = pieces
    return (dot(sel16, lo) + dot(sel16, mid)) + dot(sel16, hi)


def _silu(x):
    return x * jax.nn.sigmoid(x)


def _rope128(x, cos, sin):
    return x * cos + pltpu.roll(x, 64, 1) * sin


def _rope_idx(x, cos, sina, sinb):
    return x * cos + pltpu.roll(x, 96, 1) * sina + pltpu.roll(x, 32, 1) * sinb


def _rope_tables(pos):
    pos = np.asarray(pos, np.float64)[:, None]
    ang = pos * (ROPE_THETA ** (-np.arange(64, dtype=np.float64) / 64))[None, :]
    c, s = np.cos(ang), np.sin(ang)
    cos128 = np.concatenate([c, c], 1)
    sin128 = np.concatenate([-s, s], 1)
    ang32 = pos * (ROPE_THETA ** (-np.arange(32, dtype=np.float64) / 32))[None, :]
    c, s = np.cos(ang32), np.sin(ang32)
    z = np.zeros_like(c)
    cos_i = np.concatenate([c, c, z, z], 1)
    sin_ia = np.concatenate([-s, z, z, z], 1)
    sin_ib = np.concatenate([z, s, z, z], 1)
    return tuple(jnp.asarray(a, F32) for a in (cos128, sin128, cos_i, sin_ia, sin_ib))


def _mm_kernel(x_ref, w_ref, o_ref):
    o_ref[...] = _mm(x_ref[...], w_ref[...]).astype(o_ref.dtype)


def _matmul(x, w, out_dtype, tm, tn):
    m, k = x.shape
    n = w.shape[1]
    return pl.pallas_call(
        _mm_kernel,
        grid=(m // tm, n // tn),
        in_specs=[pl.BlockSpec((tm, k), lambda i, j: (i, 0)),
                  pl.BlockSpec((k, tn), lambda i, j: (0, j))],
        out_specs=pl.BlockSpec((tm, tn), lambda i, j: (i, j)),
        out_shape=jax.ShapeDtypeStruct((m, n), out_dtype),
        compiler_params=_cp(2),
        name="proj_matmul",
    )(x, w)


def _mm_nt_kernel(x_ref, w_ref, o_ref):
    o_ref[...] = _nt(x_ref[...], w_ref[...]).astype(o_ref.dtype)


def _matmul_nt(x, wt, out_dtype, tm, tn):
    m, k = x.shape
    n = wt.shape[0]
    return pl.pallas_call(
        _mm_nt_kernel,
        grid=(m // tm, n // tn),
        in_specs=[pl.BlockSpec((tm, k), lambda i, j: (i, 0)),
                  pl.BlockSpec((tn, k), lambda i, j: (j, 0))],
        out_specs=pl.BlockSpec((tm, tn), lambda i, j: (i, j)),
        out_shape=jax.ShapeDtypeStruct((m, n), out_dtype),
        compiler_params=_cp(2),
        name="proj_matmul_nt",
    )(x, wt)


W_PREP_COLS = 128


def _relayout_kernel(w_ref, main_ref, gate_ref):
    offs = [int(o) for o in np.concatenate([[0], np.cumsum(IN_SPLITS)])]
    cols = main_ref.shape[1]

    def put(dst, src, n):
        main_ref[dst:dst + n, :] = _b(w_ref[src:src + n, :])

    def put_padded(dst, pieces):
        parts = [w_ref[src:src + n, :] for src, n in pieces]
        parts.append(jnp.zeros((LANE - sum(n for _, n in pieces), cols), F32))
        main_ref[dst:dst + LANE, :] = _b(jnp.concatenate(parts, 0))

    put(C_CONV, offs[0], CONV_DIM)
    put(C_GDNZ, offs[1], BRANCH_W)
    put(C_DSA, offs[4], 4 * BRANCH_W)
    put(C_RET, offs[11], 4 * BRANCH_W)
    put(C_LRUZ, offs[15], BRANCH_W)
    put(C_MEM, offs[16], 2 * BRANCH_W)
    for h in range(IDX_HEADS):
        put_padded(C_QI + h * LANE, [(offs[8] + h * IDX_DIM, IDX_DIM)])
    put_padded(C_KI, [(offs[9], IDX_DIM)])
    put_padded(C_SMALL, [(offs[2], 2 * N_HEADS), (offs[10], IDX_HEADS)])
    main_ref[C_SMALL + LANE:N_MAIN, :] = jnp.zeros((N_MAIN - C_SMALL - LANE, cols), BF16)
    gate_ref[...] = _b(w_ref[offs[18]:offs[18] + N_GATE, :])


def _prep_w_in(w_in, l):
    wt = jnp.swapaxes(w_in, 1, 2)
    _, n_in, d = wt.shape
    return pl.pallas_call(
        _relayout_kernel,
        grid=(d // W_PREP_COLS,),
        in_specs=[pl.BlockSpec((None, n_in, W_PREP_COLS), lambda i: (l, 0, i))],
        out_specs=[pl.BlockSpec((N_MAIN, W_PREP_COLS), lambda i: (0, i)),
                   pl.BlockSpec((N_GATE, W_PREP_COLS), lambda i: (0, i))],
        out_shape=[jax.ShapeDtypeStruct((N_MAIN, d), BF16), jax.ShapeDtypeStruct((N_GATE, d), BF16)],
        compiler_params=_cp(1),
        name="w_in_relayout",
    )(wt)


def _ropek_kernel(k_ref, v_ref, ki_ref, cos_ref, sin_ref, ci_ref, sia_ref, sib_ref,
                  kb_ref, k16_ref, v16_ref, kif_ref, ki16_ref):
    cos, sin = cos_ref[...], sin_ref[...]
    for h in range(N_HEADS):
        sl = slice(h * HEAD_DIM, (h + 1) * HEAD_DIM)
        r = _rope128(k_ref[:, sl], cos, sin)
        kb_ref[:, sl] = r
        k16_ref[:, sl] = _b(r)
    v16_ref[...] = _b(v_ref[...])
    ki = _rope_idx(ki_ref[...], ci_ref[...], sia_ref[...], sib_ref[...])
    kif_ref[...] = ki
    ki16_ref[...] = _b(ki)


def _rope_k(h3, tabs, tr):
    bsz, t, _ = h3.shape
    cb = lambda c, w: c // w
    tab = pl.BlockSpec((tr, LANE), lambda b, i: (i, 0))
    row = lambda w, c: pl.BlockSpec((None, tr, w), lambda b, i: (b, i, c))
    return pl.pallas_call(
        _ropek_kernel,
        grid=(bsz, t // tr),
        in_specs=[row(512, cb(C_DSA + 512, 512)), row(512, cb(C_DSA + 1024, 512)), row(LANE, cb(C_KI, LANE)),
                  tab, tab, tab, tab, tab],
        out_specs=[row(512, 0), row(512, 0), row(512, 0), row(LANE, 0), row(LANE, 0)],
        out_shape=[jax.ShapeDtypeStruct((bsz, t, 512), F32), jax.ShapeDtypeStruct((bsz, t, 512), BF16),
                   jax.ShapeDtypeStruct((bsz, t, 512), BF16), jax.ShapeDtypeStruct((bsz, t, LANE), F32),
                   jax.ShapeDtypeStruct((bsz, t, LANE), BF16)],
        compiler_params=_cp(2),
        name="rope_k",
    )(h3, h3, h3, *tabs)


def _gdn_kernel(q_ref, k_ref, v_ref, z_ref, sm_ref, wq_ref, wk_ref, wv_ref, bq_ref, bk_ref, bv_ref,
                hq_ref, hk_ref, hv_ref, par_ref, s0_ref, y_ref, sfin_ref,
                scq, sck, scv, o0_s, qe_s, n_s, p_s, gl_s, *, t, c, gr):
    h = pl.program_id(1)
    nc = t // c
    cpg = gr // c
    for src, hist, scr in ((q_ref, hq_ref, scq), (k_ref, hk_ref, sck), (v_ref, hv_ref, scv)):
        scr[pl.ds(0, 8), :] = hist[...]
        scr[pl.ds(8, t), :] = src[...]
    ii = lax.broadcasted_iota(jnp.int32, (gr, gr), 0)
    jj = lax.broadcasted_iota(jnp.int32, (gr, gr), 1)
    same = (ii // c) == (jj // c)
    tril = jnp.logical_and(same, ii >= jj)
    strict = jnp.logical_and(same, ii > jj)
    tril16 = jnp.where(tril, 1.0, 0.0).astype(BF16)
    same16 = jnp.where(same, 1.0, 0.0).astype(BF16)
    lane = lax.broadcasted_iota(jnp.int32, (gr, LANE), 1)
    e0 = jnp.where(lane == 0, 1.0, 0.0).astype(BF16)
    neg_a = -jnp.exp(par_ref[0:1, 0:1])
    dt_bias = par_ref[1:2, 0:1]
    norm_w = par_ref[2:3, :]

    def conv(scr, w_ref, b_ref, r0):
        acc = b_ref[...] + scr[pl.ds(r0 + 5, gr), :] * w_ref[0:1, :]
        for j in range(1, CONV_W):
            acc = acc + scr[pl.ds(r0 + 5 + j, gr), :] * w_ref[j:j + 1, :]
        return acc

    def prep(gi, carry):
        r0 = pl.multiple_of(gi * gr, gr)
        qc = _silu(conv(scq, wq_ref, bq_ref, r0))
        kc = _silu(conv(sck, wk_ref, bk_ref, r0))
        vc = _silu(conv(scv, wv_ref, bv_ref, r0))
        qn = qc * lax.rsqrt(jnp.sum(qc * qc, -1, keepdims=True) + NORM_EPS) * SCALE
        kn = kc * lax.rsqrt(jnp.sum(kc * kc, -1, keepdims=True) + NORM_EPS)
        sm = sm_ref[pl.ds(r0, gr), :]
        a_col = jnp.sum(jnp.where(lane == h, sm, 0.0), -1, keepdims=True)
        b_col = jnp.sum(jnp.where(lane == N_HEADS + h, sm, 0.0), -1, keepdims=True)
        beta = jax.nn.sigmoid(b_col)
        gb = _split3(jnp.broadcast_to(neg_a * jax.nn.softplus(a_col + dt_bias), (gr, LANE)))
        gcb = _dot_pieces(_mm, tril16, gb)
        glb = _dot_pieces(_mm, same16, gb)
        gc = gcb[:, 0:1]
        gcr = _dot_pieces(_nt, e0, _split3(gcb))
        decay = jnp.where(tril, jnp.exp(jnp.minimum(gc - gcr, 0.0)), 0.0)
        q16, k16 = _b(qn), _b(kn)
        a = jnp.where(strict, beta * _nt(k16, k16) * decay, 0.0)
        x = -a
        r = x
        for _ in range(int(math.log2(c)) - 1):
            x16 = _b(x)
            x = _mm(x16, x16)
            r = r + x + _mm(_b(r), _b(x))
        egc = jnp.exp(gc)
        r16 = _b(r)
        rhs_v = vc * beta
        rhs_k = kn * (beta * egc)
        u = rhs_v + _mm(r16, _b(rhs_v))
        w = rhs_k + _mm(r16, _b(rhs_k))
        qk = _nt(q16, k16) * decay
        qd = qn * egc
        kt = kn * jnp.exp(glb[:, 0:1] - gc)
        egl = jnp.exp(glb)
        qk16 = _b(qk)
        qe = qd - _mm(qk16, _b(w))
        o0 = _mm(qk16, _b(u))
        for j in range(cpg):
            ci = gi * cpg + j
            rows = slice(j * c, (j + 1) * c)
            o0_s[ci] = o0[rows]
            qe_s[ci] = _b(qe[rows])
            n_s[ci] = _tn(kt[rows], u[rows])
            p_s[ci] = _b(_tn(kt[rows], w[rows]))
            gl_s[ci] = egl[j * c:j * c + 1]
        return carry

    lax.fori_loop(0, t // gr, prep, 0, unroll=math.gcd(t // gr, 2))

    def step(ci, s):
        r0 = pl.multiple_of(ci * c, c)
        s16 = _b(s)
        o = o0_s[ci] + _mm(qe_s[ci], s16)
        s = s * gl_s[ci] + (n_s[ci] - _mm(p_s[ci], s16))
        o = o * lax.rsqrt(jnp.mean(o * o, -1, keepdims=True) + NORM_EPS) * norm_w
        y_ref[pl.ds(r0, c), :] = o * _silu(z_ref[pl.ds(r0, c), :])
        return s

    sfin_ref[...] = lax.fori_loop(0, nc, step, s0_ref[...])


def _gdn(h3, hist8, conv_w, conv_b, par, s0):
    bsz, t, _ = h3.shape
    c = math.gcd(t, GDN_CHUNK)
    gr = math.gcd(t, GDN_GROUP)
    nc = t // c
    col = lambda blk: pl.BlockSpec((None, t, LANE), lambda b, h, blk=blk: (b, 0, blk + h))
    wsp = lambda blk: pl.BlockSpec((CONV_W, LANE), lambda b, h, blk=blk: (0, blk + h))
    bsp = lambda blk: pl.BlockSpec((1, LANE), lambda b, h, blk=blk: (0, blk + h))
    hsp = lambda blk: pl.BlockSpec((None, 8, LANE), lambda b, h, blk=blk: (b, 0, blk + h))
    st = pl.BlockSpec((None, None, HEAD_DIM, HEAD_DIM), lambda b, h: (b, h, 0, 0))
    return pl.pallas_call(
        functools.partial(_gdn_kernel, t=t, c=c, gr=gr),
        grid=(bsz, N_HEADS),
        in_specs=[col(0), col(4), col(8), col(C_GDNZ // LANE),
                  pl.BlockSpec((None, t, LANE), lambda b, h: (b, 0, C_SMALL // LANE)),
                  wsp(0), wsp(4), wsp(8), bsp(0), bsp(4), bsp(8), hsp(0), hsp(4), hsp(8),
                  pl.BlockSpec((None, 8, LANE), lambda b, h: (h, 0, 0)), st],
        out_specs=[pl.BlockSpec((None, t, LANE), lambda b, h: (b, 0, h)), st],
        out_shape=[jax.ShapeDtypeStruct((bsz, t, BRANCH_W), F32),
                   jax.ShapeDtypeStruct((bsz, N_HEADS, HEAD_DIM, HEAD_DIM), F32)],
        scratch_shapes=[pltpu.VMEM((t + 8, LANE), F32)] * 3 + [
            pltpu.VMEM((nc, c, LANE), F32), pltpu.VMEM((nc, c, LANE), BF16),
            pltpu.VMEM((nc, HEAD_DIM, HEAD_DIM), F32), pltpu.VMEM((nc, HEAD_DIM, HEAD_DIM), BF16),
            pltpu.VMEM((nc, 1, LANE), F32)],
        compiler_params=_cp(2),
        name="gdn",
    )(h3, h3, h3, h3, h3, conv_w, conv_w, conv_w, conv_b, conv_b, conv_b, hist8, hist8, hist8, par, s0)


def _ret_kernel(q_ref, k_ref, v_ref, z_ref, cos_ref, sin_ref, par_ref, r0_ref, y_ref, rfin_ref, *, t, c):
    nc = t // c
    lg = jnp.log(par_ref[0:1, 0:1])
    ii = lax.broadcasted_iota(jnp.int32, (c, c), 0)
    jj = lax.broadcasted_iota(jnp.int32, (c, c), 1)
    dmat = jnp.where(ii >= jj, jnp.exp((ii - jj).astype(F32) * lg), 0.0)
    icol = lax.broadcasted_iota(jnp.int32, (c, 1), 0).astype(F32)
    xi = jnp.exp((icol + 1.0) * lg)
    zeta = jnp.exp((c - 1.0 - icol) * lg)
    g_c = jnp.exp(c * lg)

    def step(ci, r):
        rows = pl.ds(pl.multiple_of(ci * c, c), c)
        cos, sin = cos_ref[rows, :], sin_ref[rows, :]
        q = _rope128(q_ref[rows, :], cos, sin)
        k = _rope128(k_ref[rows, :], cos, sin) * SCALE
        v = v_ref[rows, :]
        q16, k16, v16 = _b(q), _b(k), _b(v)
        o = _mm(_b(_nt(q16, k16) * dmat), v16) + _mm(_b(q * xi), _b(r))
        r = r * g_c + _tn(k * zeta, v)
        mu = jnp.mean(o, -1, keepdims=True)
        d = o - mu
        var = jnp.mean(d * d, -1, keepdims=True)
        y_ref[rows, :] = d * lax.rsqrt(var + LN_EPS) * _silu(z_ref[rows, :])
        return r

    rfin_ref[...] = lax.fori_loop(0, nc, step, r0_ref[...], unroll=math.gcd(nc, CHUNK_UNROLL))


def _ret_par():
    gamma = (1.0 - 2.0 ** (-5.0 - np.arange(N_HEADS, dtype=np.float64))).astype(np.float32)
    return jnp.asarray(np.broadcast_to(gamma[:, None, None], (N_HEADS, 8, LANE)).copy(), F32)


def _ret(h3, cos, sin, par, r0):
    bsz, t, _ = h3.shape
    c = math.gcd(t, RET_CHUNK)
    col = lambda blk: pl.BlockSpec((None, t, LANE), lambda b, h, blk=blk: (b, 0, blk + h))
    tab = pl.BlockSpec((t, LANE), lambda b, h: (0, 0))
    st = pl.BlockSpec((None, None, HEAD_DIM, HEAD_DIM), lambda b, h: (b, h, 0, 0))
    base = C_RET // LANE
    return pl.pallas_call(
        functools.partial(_ret_kernel, t=t, c=c),
        grid=(bsz, N_HEADS),
        in_specs=[col(base), col(base + 4), col(base + 8), col(base + 12), tab, tab,
                  pl.BlockSpec((None, 8, LANE), lambda b, h: (h, 0, 0)), st],
        out_specs=[pl.BlockSpec((None, t, LANE), lambda b, h: (b, 0, h)), st],
        out_shape=[jax.ShapeDtypeStruct((bsz, t, BRANCH_W), F32),
                   jax.ShapeDtypeStruct((bsz, N_HEADS, HEAD_DIM, HEAD_DIM), F32)],
        compiler_params=_cp(2),
        name="retention",
    )(h3, h3, h3, h3, cos, sin, par, r0)


def _lru_gates(x, wr_ref, wi_ref, br, bi, lam):
    rs, is_ = [], []
    for h in range(N_HEADS):
        sl = slice(h * HEAD_DIM, (h + 1) * HEAD_DIM)
        xh = _b(x[:, sl])
        rs.append(_mm(xh, wr_ref[h]))
        is_.append(_mm(xh, wi_ref[h]))
    r = jax.nn.sigmoid(jnp.concatenate(rs, 1) + br)
    ig = jax.nn.sigmoid(jnp.concatenate(is_, 1) + bi)
    log_a = -LRU_C * r * jax.nn.softplus(-lam)
    a = jnp.exp(log_a)
    th = jnp.tanh(log_a)
    one_m_a2 = -2.0 * th / (1.0 - th)
    return a, jnp.sqrt(one_m_a2) * (ig * x)


def _lru_kernel(x_ref, z_ref, w_ref, b_ref, hist_ref, wr_ref, wi_ref, br_ref, bi_ref, lam_ref, h0_ref,
                y_ref, hfin_ref, scx, a_s, b_s, *, t, rb):
    for h in range(N_HEADS):
        sl = slice(h * HEAD_DIM, (h + 1) * HEAD_DIM)
        scx[h, pl.ds(0, 8), :] = hist_ref[:, sl]
        scx[h, pl.ds(8, t), :] = x_ref[:, sl]
    br, bi, lam = br_ref[...], bi_ref[...], lam_ref[...]

    def gates(i, carry):
        r0 = pl.multiple_of(i * rb, rb)
        xs = []
        for h in range(N_HEADS):
            sl = slice(h * HEAD_DIM, (h + 1) * HEAD_DIM)
            xh = b_ref[:, sl] + scx[h, pl.ds(r0 + 5, rb), :] * w_ref[0:1, sl]
            for j in range(1, CONV_W):
                xh = xh + scx[h, pl.ds(r0 + 5 + j, rb), :] * w_ref[j:j + 1, sl]
            xs.append(xh)
        x = jnp.concatenate(xs, 1)
        a, b = _lru_gates(x, wr_ref, wi_ref, br, bi, lam)
        a_s[pl.ds(r0, rb), :] = a
        b_s[pl.ds(r0, rb), :] = b
        return carry

    lax.fori_loop(0, t // rb, gates, 0)

    def scan(i, h):
        h = a_s[pl.ds(i, 1), :] * h + b_s[pl.ds(i, 1), :]
        b_s[pl.ds(i, 1), :] = h
        return h

    hfin_ref[...] = lax.fori_loop(0, t, scan, h0_ref[...], unroll=8)

    def gate_out(i, carry):
        rows = pl.ds(pl.multiple_of(i * rb, rb), rb)
        y_ref[rows, :] = b_s[rows, :] * _silu(z_ref[rows, :])
        return carry

    lax.fori_loop(0, t // rb, gate_out, 0)


def _lru(h3, hist8, conv_w, conv_b, w_r, w_i, b_r, b_i, lam, h0):
    bsz, t, _ = h3.shape
    rb = math.gcd(t, 256)
    xblk = (C_CONV + 3 * BRANCH_W) // BRANCH_W
    vec = pl.BlockSpec((1, BRANCH_W), lambda b: (0, 0))
    wsp = pl.BlockSpec((N_HEADS, HEAD_DIM, HEAD_DIM), lambda b: (0, 0, 0))
    return pl.pallas_call(
        functools.partial(_lru_kernel, t=t, rb=rb),
        grid=(bsz,),
        in_specs=[pl.BlockSpec((None, t, BRANCH_W), lambda b: (b, 0, xblk)),
                  pl.BlockSpec((None, t, BRANCH_W), lambda b: (b, 0, C_LRUZ // BRANCH_W)),
                  pl.BlockSpec((CONV_W, BRANCH_W), lambda b: (0, xblk)),
                  pl.BlockSpec((1, BRANCH_W), lambda b: (0, xblk)),
                  pl.BlockSpec((None, 8, BRANCH_W), lambda b: (b, 0, xblk)),
                  wsp, wsp, vec, vec, vec,
                  pl.BlockSpec((None, 1, BRANCH_W), lambda b: (b, 0, 0))],
        out_specs=[pl.BlockSpec((None, t, BRANCH_W), lambda b: (b, 0, 0)),
                   pl.BlockSpec((None, 1, BRANCH_W), lambda b: (b, 0, 0))],
        out_shape=[jax.ShapeDtypeStruct((bsz, t, BRANCH_W), F32), jax.ShapeDtypeStruct((bsz, 1, BRANCH_W), F32)],
        scratch_shapes=[pltpu.VMEM((N_HEADS, t + 8, HEAD_DIM), F32), pltpu.VMEM((t, BRANCH_W), F32),
                        pltpu.VMEM((t, BRANCH_W), F32)],
        compiler_params=_cp(1),
        name="rglru",
    )(h3, h3, conv_w, conv_b, hist8, w_r, w_i, b_r, b_i, lam, h0)


CHUNK_UNROLL = 4
BISECT_STEPS = 18


def _kth_threshold(count_ge, count_gt, next_above, lo, hi, k):
    c_hi = count_ge(hi)
    top = c_hi >= k
    lo = jnp.where(top, hi, lo)

    def halve(_, c):
        lo, hi = c
        mid = lo + (hi - lo) * 0.5
        up = count_ge(mid) >= k
        return jnp.where(up, mid, lo), jnp.where(up, hi, mid)

    lo, _ = lax.fori_loop(0, BISECT_STEPS, halve, (lo, hi))

    def more(c):
        return jnp.max(c[1] - k) > 0.5

    def walk(c):
        thr, c_gt = c
        nxt = jnp.where(c_gt > k, next_above(thr), thr)
        return nxt, count_gt(nxt)

    return lax.while_loop(more, walk, (lo, count_gt(lo)))


def _dsa_kernel(q_ref, z_ref, qi_ref, sm_ref, cos_ref, sin_ref, ci_ref, sia_ref, sib_ref,
                k_ref, v_ref, ki_ref, tri_ref, o_ref, sc_s, bias_s, *, tq, t, ksel, tile0):
    i = pl.program_id(1) + tile0
    ki16 = ki_ref[...]
    sm = sm_ref[...]
    ci, sia, sib = ci_ref[...], sia_ref[...], sib_ref[...]
    score = jnp.zeros((tq, t), F32)
    for h in range(IDX_HEADS):
        x = _rope_idx(qi_ref[:, h * LANE:(h + 1) * LANE], ci, sia, sib)
        s = _nt(_b(x), ki16)
        w = sm[:, 8 + h:9 + h] * IDX_SCALE
        score = score + jnp.maximum(s, 0.0) * w
    qpos = i * tq + lax.broadcasted_iota(jnp.int32, (tq, 1), 0)
    kpos = lax.broadcasted_iota(jnp.int32, (1, t), 1)
    valid = kpos <= qpos
    sc_s[...] = jnp.where(valid, score, -jnp.inf)
    k_row = jnp.minimum(float(ksel), (qpos + 1).astype(F32))

    def count_ge(th):
        return jnp.sum(jnp.where(sc_s[...] >= th, 1.0, 0.0), -1, keepdims=True)

    def count_gt(th):
        return jnp.sum(jnp.where(sc_s[...] > th, 1.0, 0.0), -1, keepdims=True)

    def next_above(th):
        s = sc_s[...]
        return jnp.min(jnp.where(s > th, s, jnp.inf), -1, keepdims=True)

    lo = jnp.min(jnp.where(valid, score, jnp.inf), -1, keepdims=True)
    hi = jnp.max(sc_s[...], -1, keepdims=True)
    thr, n_gt = _kth_threshold(count_ge, count_gt, next_above, lo, hi, k_row)
    need = k_row - n_gt
    run = jnp.zeros((tq, 1), F32)
    tri = tri_ref[...]
    for cb in range(t // LANE):
        sl = slice(cb * LANE, (cb + 1) * LANE)
        sc = sc_s[:, sl]
        eq = sc == thr
        pre = _mm(jnp.where(eq, 1.0, 0.0).astype(BF16), tri) + run
        bias_s[:, sl] = jnp.where(sc > thr, 0.0, jnp.where(eq, jnp.where(pre <= need, 0.0, NEG), NEG))
        run = pre[:, LANE - 1:LANE]
    bias = bias_s[...]
    cos, sin = cos_ref[...], sin_ref[...]
    for h in range(N_HEADS):
        sl = slice(h * HEAD_DIM, (h + 1) * HEAD_DIM)
        q = _rope128(q_ref[:, sl], cos, sin)
        lg = _nt(_b(q), k_ref[:, sl]) * SCALE + bias
        m = jnp.max(lg, -1, keepdims=True)
        p = jnp.exp(lg - m)
        l = jnp.sum(p, -1, keepdims=True)
        o = _mm(_b(p), v_ref[:, sl]) / l
        o_ref[:, sl] = o * _silu(z_ref[:, sl])


def _tri128():
    return jnp.asarray(np.triu(np.ones((LANE, LANE), np.float32)), BF16)


DSA_KEY_CLASSES = 8


def _dsa_class(h3, k16, v16, ki16, tabs, tq, tile0, n_tiles, t_keys, ksel):
    bsz = h3.shape[0]
    cos, sin, ci, sia, sib = tabs
    qtab = pl.BlockSpec((tq, LANE), lambda b, i: (i + tile0, 0))
    keys = lambda w: pl.BlockSpec((None, t_keys, w), lambda b, i: (b, 0, 0))
    return pl.pallas_call(
        functools.partial(_dsa_kernel, tq=tq, t=t_keys, ksel=ksel, tile0=tile0),
        grid=(bsz, n_tiles),
        in_specs=[pl.BlockSpec((None, tq, 512), lambda b, i: (b, i + tile0, C_DSA // 512)),
                  pl.BlockSpec((None, tq, 512), lambda b, i: (b, i + tile0, (C_DSA + 1536) // 512)),
                  pl.BlockSpec((None, tq, 1024), lambda b, i: (b, i + tile0, C_QI // 1024)),
                  pl.BlockSpec((None, tq, LANE), lambda b, i: (b, i + tile0, C_SMALL // LANE)),
                  qtab, qtab, qtab, qtab, qtab,
                  keys(512), keys(512), keys(LANE),
                  pl.BlockSpec((LANE, LANE), lambda b, i: (0, 0))],
        out_specs=pl.BlockSpec((None, tq, 512), lambda b, i: (b, i, 0)),
        out_shape=jax.ShapeDtypeStruct((bsz, n_tiles * tq, BRANCH_W), F32),
        scratch_shapes=[pltpu.VMEM((tq, t_keys), F32), pltpu.VMEM((tq, t_keys), F32)],
        compiler_params=_cp(2),
        name="dsa_prompt",
    )(h3, h3, h3, h3, cos, sin, ci, sia, sib, k16, v16, ki16, _tri128())


def _dsa(h3, k16, v16, ki16, tabs, tq):
    _, t, _ = h3.shape
    ksel = min(DSA_TOPK, t // 4)
    n_tiles = t // tq
    n_cls = math.gcd(n_tiles, DSA_KEY_CLASSES)
    per = n_tiles // n_cls
    outs = [_dsa_class(h3, k16, v16, ki16, tabs, tq, c * per, per, (c + 1) * per * tq, ksel) for c in range(n_cls)]
    return jnp.concatenate(outs, 1)


def _mem_kernel(q_ref, z_ref, mk_ref, mv_ref, o_ref):
    for h in range(N_HEADS):
        sl = slice(h * HEAD_DIM, (h + 1) * HEAD_DIM)
        lg = _nt(_b(q_ref[:, sl]), _b(mk_ref[:, sl])) * SCALE
        m = jnp.max(lg, -1, keepdims=True)
        p = jnp.exp(lg - m)
        l = jnp.sum(p, -1, keepdims=True)
        o = _mm(_b(p), _b(mv_ref[:, sl])) / l
        o_ref[:, sl] = o * _silu(z_ref[:, sl])


def _mem(h3, mkv3, tq):
    bsz, t, _ = h3.shape
    nm = mkv3.shape[1]
    return pl.pallas_call(
        _mem_kernel,
        grid=(bsz, t // tq),
        in_specs=[pl.BlockSpec((None, tq, 512), lambda b, i: (b, i, C_MEM // 512)),
                  pl.BlockSpec((None, tq, 512), lambda b, i: (b, i, C_MEM // 512 + 1)),
                  pl.BlockSpec((None, nm, 512), lambda b, i: (b, 0, 0)),
                  pl.BlockSpec((None, nm, 512), lambda b, i: (b, 0, 1))],
        out_specs=pl.BlockSpec((None, tq, 512), lambda b, i: (b, i, 0)),
        out_shape=jax.ShapeDtypeStruct((bsz, t, BRANCH_W), F32),
        compiler_params=_cp(2),
        name="mem_attn",
    )(h3, h3, mkv3, mkv3)


def _merge_kernel(ya_ref, yb_ref, yc_ref, yd_ref, ye_ref, g_ref, w_ref, o_ref):
    acc = None
    for nbr, y_ref in enumerate((ya_ref, yb_ref, yc_ref, yd_ref, ye_ref)):
        p = _mm(_b(y_ref[...]), w_ref[nbr])
        g = jax.nn.sigmoid(g_ref[:, nbr * D_MODEL:(nbr + 1) * D_MODEL].astype(F32))
        acc = g * p if acc is None else acc + g * p
    o_ref[...] = _b(acc)


def _merge(ys, g, w_br, tm):
    m = g.shape[0]
    ysp = pl.BlockSpec((tm, BRANCH_W), lambda i: (i, 0))
    return pl.pallas_call(
        _merge_kernel,
        grid=(m // tm,),
        in_specs=[ysp] * N_BRANCH + [pl.BlockSpec((tm, N_GATE), lambda i: (i, 0)),
                                     pl.BlockSpec((N_BRANCH, BRANCH_W, D_MODEL), lambda i: (0, 0, 0),
                                                  pipeline_mode=pl.Buffered(1))],
        out_specs=pl.BlockSpec((tm, D_MODEL), lambda i: (i, 0)),
        out_shape=jax.ShapeDtypeStruct((m, D_MODEL), BF16),
        compiler_params=_cp(1),
        name="gated_merge",
    )(*ys, g, w_br)


def _outnorm_kernel(m_ref, x_ref, w_ref, g_ref, b_ref, y_ref, y16_ref):
    v = DEEPNORM_ALPHA * x_ref[...] + _mm(m_ref[...], w_ref[...])
    mu = jnp.mean(v, -1, keepdims=True)
    d = v - mu
    var = jnp.mean(d * d, -1, keepdims=True)
    y = d * lax.rsqrt(var + LN_EPS) * g_ref[...] + b_ref[...]
    y_ref[...] = y
    y16_ref[...] = _b(y)


def _outnorm(merged, x, w_out, ln_g, ln_b, tm):
    m = x.shape[0]
    row = pl.BlockSpec((tm, D_MODEL), lambda i: (i, 0))
    vec = pl.BlockSpec((1, D_MODEL), lambda i: (0, 0))
    return pl.pallas_call(
        _outnorm_kernel,
        grid=(m // tm,),
        in_specs=[row, row, pl.BlockSpec((D_MODEL, D_MODEL), lambda i: (0, 0)), vec, vec],
        out_specs=[row, row],
        out_shape=[jax.ShapeDtypeStruct((m, D_MODEL), F32), jax.ShapeDtypeStruct((m, D_MODEL), BF16)],
        compiler_params=_cp(1),
        name="out_norm",
    )(merged, x, w_out, ln_g, ln_b)


def _layer_params(l, w_in, conv_w, conv_b, gdn_a_log, gdn_dt_bias, gdn_norm_w, lru_lambda, lru_w_r, lru_b_r,
                  lru_w_i, lru_b_i, w_mem_kv, w_branch, w_out, ln_g, ln_b):
    w_main, w_gate = _prep_w_in(w_in, l)
    rows = jnp.stack([jnp.broadcast_to(gdn_a_log[l][:, None], (N_HEADS, LANE)),
                      jnp.broadcast_to(gdn_dt_bias[l][:, None], (N_HEADS, LANE)),
                      jnp.broadcast_to(gdn_norm_w[l][None, :], (N_HEADS, LANE))], 1)
    gdn_par = jnp.concatenate([rows, jnp.zeros((N_HEADS, 5, LANE), F32)], 1)
    return dict(
        w_main=w_main, w_gate=w_gate, conv_w=conv_w[l], conv_b=conv_b[l][None, :], gdn_par=gdn_par,
        ret_par=_ret_par(), lru_w_r=_b(lru_w_r[l]), lru_w_i=_b(lru_w_i[l]),
        lru_b_r=lru_b_r[l][None, :], lru_b_i=lru_b_i[l][None, :], lam=lru_lambda[l][None, :],
        w_mem_kv=_b(w_mem_kv[l]), w_branch=_b(w_branch[l]), w_out=_b(w_out[l]),
        ln_g=ln_g[l][None, :], ln_b=ln_b[l][None, :])


def _hist8(buf):
    return jnp.concatenate([jnp.zeros((buf.shape[0], 8 - (CONV_W - 1), buf.shape[2]), buf.dtype), buf], 1)


def _prompt_layer(x, x16, mem16, p, tabs):
    bsz, t, _ = x.shape
    m = bsz * t
    tm = math.gcd(m, 1024)
    h = _matmul_nt(x16, p["w_main"], F32, tm, 512)
    g = _matmul_nt(x16, p["w_gate"], BF16, tm, 1024)
    mkv = _matmul(mem16, p["w_mem_kv"], F32, math.gcd(mem16.shape[0], 512), 512)
    h3 = h.reshape(bsz, t, N_MAIN)
    mkv3 = mkv.reshape(bsz, -1, 2 * BRANCH_W)
    zero_hist = jnp.zeros((bsz, 8, CONV_DIM), F32)
    zero_state = jnp.zeros((bsz, N_HEADS, HEAD_DIM, HEAD_DIM), F32)
    cos, sin = tabs[0], tabs[1]
    y_a, s_gdn = _gdn(h3, zero_hist, p["conv_w"], p["conv_b"], p["gdn_par"], zero_state)
    kb, k16, v16, kif, ki16 = _rope_k(h3, tabs, math.gcd(t, 512))
    y_b = _dsa(h3, k16, v16, ki16, tabs, math.gcd(t, 256))
    y_c, s_ret = _ret(h3, cos, sin, p["ret_par"], zero_state)
    y_d, s_lru = _lru(h3, zero_hist, p["conv_w"], p["conv_b"], p["lru_w_r"], p["lru_w_i"], p["lru_b_r"],
                      p["lru_b_i"], p["lam"], jnp.zeros((bsz, 1, BRANCH_W), F32))
    y_e = _mem(h3, mkv3, math.gcd(t, 512))
    ys = [a.reshape(m, BRANCH_W) for a in (y_a, y_b, y_c, y_d, y_e)]
    merged = _merge(ys, g, p["w_branch"], math.gcd(m, 512))
    y, y16 = _outnorm(merged, x.reshape(m, D_MODEL), p["w_out"], p["ln_g"], p["ln_b"], math.gcd(m, 512))
    new_conv = h3[:, t - (CONV_W - 1):, C_CONV:C_CONV + CONV_DIM]
    v_rows = h3[:, :, C_DSA + 1024:C_DSA + 1536].reshape(bsz, t, N_HEADS, HEAD_DIM)
    mk = mkv3[:, :, :BRANCH_W].reshape(bsz, -1, N_HEADS, HEAD_DIM)
    mv = mkv3[:, :, BRANCH_W:].reshape(bsz, -1, N_HEADS, HEAD_DIM)
    return (y.reshape(bsz, t, D_MODEL), y16,
            (new_conv, s_gdn, s_ret, s_lru.reshape(bsz, BRANCH_W)),
            (kb.reshape(bsz, t, N_HEADS, HEAD_DIM), v_rows, kif[:, :, :IDX_DIM]), mk, mv)


def _dec_kernel(hs_ref, hist_ref, cw_ref, cb_ref, gpar_ref, rpar_ref, sg_ref, sr_ref, hl_ref,
                wr_ref, wi_ref, br_ref, bi_ref, lam_ref, mk_ref, mv_ref,
                cos_ref, sin_ref, ci_ref, sia_ref, sib_ref,
                ya_ref, yc_ref, yd_ref, ye_ref, sgo_ref, sro_ref, hlo_ref, qb_ref, kb_ref, qi_ref, ki_ref, ss_ref,
                wk_s, vb_s, qd_s, qkg_s, eg_s, kg_s, qx_s, qkr_s, kr_s, vr_s, mq_s, oa_s, oc_s, oe_s, *, nb):
    hd = HEAD_DIM
    hsl = lambda h: slice(h * hd, (h + 1) * hd)
    x_in = hs_ref[:, C_CONV:C_CONV + CONV_DIM]
    conv = cb_ref[...] + x_in * cw_ref[CONV_W - 1:CONV_W, :]
    for j in range(CONV_W - 1):
        conv = conv + hist_ref[j] * cw_ref[j:j + 1, :]
    sm = hs_ref[:, C_SMALL:C_SMALL + LANE]
    cos, sin = cos_ref[...], sin_ref[...]
    ci, sia, sib = ci_ref[...], sia_ref[...], sib_ref[...]
    ones = jnp.ones((1, hd), F32)
    for h in range(N_HEADS):
        sl = hsl(h)
        q = _silu(conv[:, h * hd:(h + 1) * hd])
        k = _silu(conv[:, BRANCH_W + h * hd:BRANCH_W + (h + 1) * hd])
        v = _silu(conv[:, 2 * BRANCH_W + h * hd:2 * BRANCH_W + (h + 1) * hd])
        q = q * lax.rsqrt(jnp.sum(q * q, -1, keepdims=True) + NORM_EPS) * SCALE
        k = k * lax.rsqrt(jnp.sum(k * k, -1, keepdims=True) + NORM_EPS)
        beta = jax.nn.sigmoid(sm[:, N_HEADS + h:N_HEADS + h + 1])
        g = -jnp.exp(gpar_ref[h, 0:1, 0:1]) * jax.nn.softplus(sm[:, h:h + 1] + gpar_ref[h, 1:2, 0:1])
        eg = jnp.exp(g)
        wk_s[h] = k * (beta * eg)
        vb_s[h] = v * beta
        qd_s[h] = q * eg
        qkg_s[h] = jnp.sum(q * k, -1, keepdims=True) * ones
        eg_s[h] = eg * ones
        kg_s[h] = k
        q = _rope128(hs_ref[:, C_RET + h * hd:C_RET + (h + 1) * hd], cos, sin)
        k = _rope128(hs_ref[:, C_RET + BRANCH_W + h * hd:C_RET + BRANCH_W + (h + 1) * hd], cos, sin) * SCALE
        qx_s[h] = q * rpar_ref[h, 0:1, 0:1]
        qkr_s[h] = jnp.sum(q * k, -1, keepdims=True) * ones
        kr_s[h] = k
        vr_s[h] = hs_ref[:, C_RET + 2 * BRANCH_W + h * hd:C_RET + 2 * BRANCH_W + (h + 1) * hd]
        mq_s[h] = hs_ref[:, C_MEM + h * hd:C_MEM + (h + 1) * hd]
        qb_ref[:, sl] = _rope128(hs_ref[:, C_DSA + h * hd:C_DSA + (h + 1) * hd], cos, sin)
        kb_ref[:, sl] = _rope128(hs_ref[:, C_DSA + BRANCH_W + h * hd:C_DSA + BRANCH_W + (h + 1) * hd], cos, sin)
    ki = _rope_idx(hs_ref[:, C_KI:C_KI + LANE], ci, sia, sib)
    ki_ref[...] = ki
    sself = jnp.zeros((nb, 1), F32)
    for h in range(IDX_HEADS):
        qi = _rope_idx(hs_ref[:, C_QI + h * LANE:C_QI + (h + 1) * LANE], ci, sia, sib)
        qi_ref[:, h * LANE:(h + 1) * LANE] = qi
        s = jnp.sum(qi * ki, -1, keepdims=True)
        sself = sself + jnp.maximum(s, 0.0) * (sm[:, 8 + h:9 + h] * IDX_SCALE)
    ss_ref[...] = sself * jnp.ones((1, LANE), F32)

    ii = lax.broadcasted_iota(jnp.int32, (hd, hd), 0)
    jj = lax.broadcasted_iota(jnp.int32, (hd, hd), 1)
    eye = jnp.where(ii == jj, 1.0, 0.0).astype(F32)

    def col_bcast(row):
        return _nt(eye, jnp.broadcast_to(row, (hd, hd)), HI)

    def per_batch(bb, carry):
        r = pl.ds(bb, 1)
        for h in range(N_HEADS):
            sl = hsl(h)
            s = sg_ref[bb, h]
            v_new = vb_s[h, r, :] - _mm(wk_s[h, r, :], s, HI)
            oa_s[h, r, :] = _mm(qd_s[h, r, :], s, HI) + qkg_s[h, r, :] * v_new
            sgo_ref[bb, h] = s * eg_s[h, r, :] + col_bcast(kg_s[h, r, :]) * v_new
            rs = sr_ref[bb, h]
            v = vr_s[h, r, :]
            oc_s[h, r, :] = _mm(qx_s[h, r, :], rs, HI) + qkr_s[h, r, :] * v
            sro_ref[bb, h] = rs * rpar_ref[h, 0:1, :] + col_bcast(kr_s[h, r, :]) * v
            lg = _nt(_b(mq_s[h, r, :]), _b(mk_ref[bb, :, sl])) * SCALE
            m = jnp.max(lg, -1, keepdims=True)
            p = jnp.exp(lg - m)
            oe_s[h, r, :] = _mm(_b(p), _b(mv_ref[bb, :, sl])) / jnp.sum(p, -1, keepdims=True)
        return carry

    lax.fori_loop(0, nb, per_batch, 0)

    for h in range(N_HEADS):
        sl = hsl(h)
        o = oa_s[h]
        o = o * lax.rsqrt(jnp.mean(o * o, -1, keepdims=True) + NORM_EPS) * gpar_ref[h, 2:3, :]
        ya_ref[:, sl] = o * _silu(hs_ref[:, C_GDNZ + h * hd:C_GDNZ + (h + 1) * hd])
        o = oc_s[h]
        d = o - jnp.mean(o, -1, keepdims=True)
        var = jnp.mean(d * d, -1, keepdims=True)
        yc_ref[:, sl] = d * lax.rsqrt(var + LN_EPS) * _silu(hs_ref[:, C_RET + 3 * BRANCH_W + h * hd:C_RET + 3 * BRANCH_W + (h + 1) * hd])
        ye_ref[:, sl] = oe_s[h] * _silu(hs_ref[:, C_MEM + BRANCH_W + h * hd:C_MEM + BRANCH_W + (h + 1) * hd])
    a, b = _lru_gates(conv[:, 3 * BRANCH_W:], wr_ref, wi_ref, br_ref[...], bi_ref[...], lam_ref[...])
    h_new = a * hl_ref[...] + b
    hlo_ref[...] = h_new
    yd_ref[...] = h_new * _silu(hs_ref[:, C_LRUZ:C_LRUZ + BRANCH_W])


def _decode(hs, hist_t, p, sg, sr, hl, mk3, mv3, tabs):
    nb = hs.shape[0]
    bw = jax.ShapeDtypeStruct((nb, BRANCH_W), F32)
    st = jax.ShapeDtypeStruct((nb, N_HEADS, HEAD_DIM, HEAD_DIM), F32)
    out_shape = [bw, bw, bw, bw, st, st, bw, bw, bw,
                 jax.ShapeDtypeStruct((nb, IDX_HEADS * LANE), F32), jax.ShapeDtypeStruct((nb, LANE), F32),
                 jax.ShapeDtypeStruct((nb, LANE), F32)]
    return pl.pallas_call(
        functools.partial(_dec_kernel, nb=nb),
        out_shape=out_shape,
        scratch_shapes=[pltpu.VMEM((N_HEADS, nb, HEAD_DIM), F32)] * 14,
        compiler_params=pltpu.CompilerParams(vmem_limit_bytes=VMEM_LIMIT),
        name="decode_step",
    )(hs, hist_t, p["conv_w"], p["conv_b"], p["gdn_par"], p["ret_par"], sg, sr, hl,
      p["lru_w_r"], p["lru_w_i"], p["lru_b_r"], p["lru_b_i"], p["lam"], mk3, mv3, *tabs)


PAGES_PER_STEP = 16


IDX_PAGES_PER_STEP = 32


def _didx_kernel(pt_ref, qi_ref, w_ref, *rest):
    pages, o_ref = rest[:-1], rest[-1]
    q16 = _b(qi_ref[:, :IDX_DIM])
    w = w_ref[...]
    for j, pg in enumerate(pages):
        s = _mm(q16, _b(pg[...]))
        o_ref[j:j + 1, :] = jnp.sum(jnp.maximum(s, 0.0) * w, 0, keepdims=True)


def _didx(page_table, qi3, wcol, pool_ikt, l):
    nb, n_pages = page_table.shape
    pps = math.gcd(n_pages, IDX_PAGES_PER_STEP)
    ng = n_pages // pps
    page = lambda j: pl.BlockSpec((None, None, IDX_DIM, PAGE_SIZE),
                                  lambda b, g, pt, j=j: (l, pt[b, g * pps + j], 0, 0))
    return pl.pallas_call(
        _didx_kernel,
        grid_spec=pltpu.PrefetchScalarGridSpec(
            num_scalar_prefetch=1, grid=(nb, ng),
            in_specs=[pl.BlockSpec((None, IDX_HEADS, LANE), lambda b, g, pt: (b, 0, 0)),
                      pl.BlockSpec((None, IDX_HEADS, 1), lambda b, g, pt: (b, 0, 0))]
            + [page(j) for j in range(pps)],
            out_specs=pl.BlockSpec((None, None, pps, PAGE_SIZE), lambda b, g, pt: (b, g, 0, 0))),
        out_shape=jax.ShapeDtypeStruct((nb, ng, pps, PAGE_SIZE), F32),
        compiler_params=_cp(2),
        name="paged_indexer",
    )(page_table, qi3, wcol, *([pool_ikt] * pps))


NEG_SEL = -2e30
M_INIT = -1e30


def _dsel_kernel(sc_ref, ss_ref, tri_ref, bias_ref, sbias_ref, *, n_pages, ksel):
    key = sc_ref[...]
    kself = ss_ref[:, 0:1]

    def reduce2(x, op):
        return op(op(x, 1, keepdims=True), 0, keepdims=True)

    def count_ge(th):
        return reduce2(jnp.where(key >= th, 1.0, 0.0), jnp.sum) + jnp.where(kself >= th, 1.0, 0.0)

    def count_gt(th):
        return reduce2(jnp.where(key > th, 1.0, 0.0), jnp.sum) + jnp.where(kself > th, 1.0, 0.0)

    def next_above(th):
        return jnp.minimum(reduce2(jnp.where(key > th, key, jnp.inf), jnp.min), jnp.where(kself > th, kself, jnp.inf))

    lo = jnp.minimum(reduce2(key, jnp.min), kself)
    hi = jnp.maximum(reduce2(key, jnp.max), kself)
    thr, n_gt = _kth_threshold(count_ge, count_gt, next_above, lo, hi, float(ksel))
    eq = key == thr
    need = ksel - n_gt
    pre_in = _mm(jnp.where(eq, 1.0, 0.0).astype(BF16), tri_ref[...])
    tot = pre_in[:, PAGE_SIZE - 1:PAGE_SIZE]
    ii = lax.broadcasted_iota(jnp.int32, (n_pages, n_pages), 0)
    jj = lax.broadcasted_iota(jnp.int32, (n_pages, n_pages), 1)
    below = jnp.where(ii > jj, 1.0, 0.0).astype(BF16)
    pre = pre_in + _mm(below, _b(jnp.broadcast_to(tot, (n_pages, PAGE_SIZE))))
    bias_ref[...] = jnp.where(key > thr, 0.0, jnp.where(eq, jnp.where(pre <= need, 0.0, NEG_SEL), NEG_SEL))
    n_eq = pre[n_pages - 1:n_pages, PAGE_SIZE - 1:PAGE_SIZE]
    sb = jnp.where(kself > thr, 0.0, jnp.where(kself == thr, jnp.where(n_eq + 1.0 <= need, 0.0, NEG_SEL), NEG_SEL))
    sbias_ref[...] = sb * jnp.ones((1, LANE), F32)


def _dsel(scores3, sself3):
    nb, n_pages, _ = scores3.shape
    ksel = min(DSA_TOPK, (n_pages * PAGE_SIZE + 1) // 4)
    return pl.pallas_call(
        functools.partial(_dsel_kernel, n_pages=n_pages, ksel=ksel),
        grid=(nb,),
        in_specs=[pl.BlockSpec((None, n_pages, PAGE_SIZE), lambda b: (b, 0, 0)),
                  pl.BlockSpec((None, 1, LANE), lambda b: (b, 0, 0)),
                  pl.BlockSpec((LANE, LANE), lambda b: (0, 0))],
        out_specs=[pl.BlockSpec((None, n_pages, PAGE_SIZE), lambda b: (b, 0, 0)),
                   pl.BlockSpec((None, 1, LANE), lambda b: (b, 0, 0))],
        out_shape=[jax.ShapeDtypeStruct((nb, n_pages, PAGE_SIZE), F32), jax.ShapeDtypeStruct((nb, 1, LANE), F32)],
        compiler_params=_cp(1),
        name="sample_select",
    )(scores3, sself3, _tri128())


def _datt_kernel(pt_ref, q_ref, bias_ref, sbias_ref, kself_ref, vself_ref, z_ref, *rest, ng, pps):
    kpages = rest[:pps]
    vpages = rest[pps:2 * pps]
    o_ref, m_s, l_s, acc_s = rest[2 * pps:]
    g = pl.program_id(1)

    @pl.when(g == 0)
    def _():
        m_s[...] = jnp.full(m_s.shape, M_INIT, F32)
        l_s[...] = jnp.zeros(l_s.shape, F32)
        acc_s[...] = jnp.zeros(acc_s.shape, F32)

    rows = PAGE_SIZE * N_HEADS
    sub = lax.broadcasted_iota(jnp.int32, (8, rows), 0)
    own = lax.broadcasted_iota(jnp.int32, (8, rows), 1) % N_HEADS == sub
    q8 = q_ref[...]
    q16 = _b(q8)

    def update(lg, pv):
        m_prev = m_s[...]
        m_new = jnp.maximum(m_prev, jnp.max(lg, -1, keepdims=True))
        alpha = jnp.exp(m_prev - m_new)
        p = jnp.exp(lg - m_new)
        l_s[...] = l_s[...] * alpha + jnp.sum(p, -1, keepdims=True)
        acc_s[...] = acc_s[...] * alpha + pv(p)
        m_s[...] = m_new

    lg = jnp.concatenate(
        [jnp.where(own, _nt(q16, _b(kpages[j][...])) * SCALE + bias_ref[j:j + 1, :], NEG_SEL)
         for j in range(pps)], 1)

    def pv_pages(p):
        acc = _mm(_b(p[:, 0:rows]), _b(vpages[0][...]))
        for j in range(1, pps):
            acc = acc + _mm(_b(p[:, j * rows:(j + 1) * rows]), _b(vpages[j][...]))
        return acc

    update(lg, pv_pages)

    @pl.when(g == ng - 1)
    def _():
        lgs = jnp.sum(q8 * kself_ref[...], -1, keepdims=True) * SCALE + sbias_ref[:, 0:1]
        update(lgs, lambda p: p * vself_ref[...])
        l = l_s[...]
        o_ref[...] = acc_s[...] / jnp.where(l > 0.0, l, 1.0) * _silu(z_ref[...])


def _datt(page_table, q8, bias4, sbias3, kself8, vself8, z8, pool_k, pool_v, l):
    nb, n_pages = page_table.shape
    pps = math.gcd(n_pages, PAGES_PER_STEP)
    ng = n_pages // pps
    rows = PAGE_SIZE * N_HEADS
    head = pl.BlockSpec((None, 8, HEAD_DIM), lambda b, g, pt: (b, 0, 0))
    page = lambda j: pl.BlockSpec((None, None, rows, HEAD_DIM),
                                  lambda b, g, pt, j=j: (l, pt[b, g * pps + j], 0, 0))
    return pl.pallas_call(
        functools.partial(_datt_kernel, ng=ng, pps=pps),
        grid_spec=pltpu.PrefetchScalarGridSpec(
            num_scalar_prefetch=1, grid=(nb, ng),
            in_specs=[head, pl.BlockSpec((None, pps, rows), lambda b, g, pt: (b, g, 0)),
                      pl.BlockSpec((None, 1, LANE), lambda b, g, pt: (b, 0, 0)), head, head, head]
            + [page(j) for j in range(pps)] * 2,
            out_specs=head,
            scratch_shapes=[pltpu.VMEM((8, 1), F32), pltpu.VMEM((8, 1), F32), pltpu.VMEM((8, HEAD_DIM), F32)]),
        out_shape=jax.ShapeDtypeStruct((nb, 8, HEAD_DIM), F32),
        compiler_params=_cp(2),
        name="paged_attention",
    )(page_table, q8, bias4, sbias3, kself8, vself8, z8, *([pool_k] * pps), *([pool_v] * pps))


def _sample_layer(x, x16, p, tabs, conv_buf, sg, sr, hl, mem_k, mem_v, pool_k, pool_v, pool_ik, page_table, l):
    nb = x.shape[0]
    hs = _matmul_nt(x16, p["w_main"], F32, nb, 512)
    gs = _matmul_nt(x16, p["w_gate"], BF16, nb, 1024)
    mk3 = mem_k.reshape(nb, -1, BRANCH_W)
    mv3 = mem_v.reshape(nb, -1, BRANCH_W)
    (y_a, y_c, y_d, y_e, sg_new, sr_new, hl_new, qb, kb, qi, ki, sself) = _decode(
        hs, conv_buf.transpose(1, 0, 2), p, sg, sr, hl, mk3, mv3, tabs)
    wcol = (hs[:, C_SMALL + 8:C_SMALL + 16] * IDX_SCALE)[:, :, None]
    scores = _didx(page_table, qi.reshape(nb, IDX_HEADS, LANE), wcol, jnp.swapaxes(pool_ik, 2, 3), l)
    n_pages = page_table.shape[1]
    bias3, sbias3 = _dsel(scores.reshape(nb, n_pages, PAGE_SIZE), sself[:, None, :])
    v_self = hs[:, C_DSA + 1024:C_DSA + 1536]

    def heads8(a):
        a = a.reshape(nb, N_HEADS, HEAD_DIM)
        return jnp.concatenate([a, jnp.zeros((nb, 8 - N_HEADS, HEAD_DIM), a.dtype)], 1)

    pages = lambda pool: pool.reshape(*pool.shape[:2], PAGE_SIZE * N_HEADS, HEAD_DIM)
    y_b = _datt(page_table, heads8(qb), jnp.repeat(bias3, N_HEADS, axis=-1), sbias3, heads8(kb), heads8(v_self),
                heads8(hs[:, C_DSA + 1536:C_DSA + 2048]), pages(pool_k), pages(pool_v), l)
    merged = _merge([y_a, y_b[:, :N_HEADS].reshape(nb, BRANCH_W), y_c, y_d, y_e], gs, p["w_branch"], nb)
    y, y16 = _outnorm(merged, x.reshape(nb, D_MODEL), p["w_out"], p["ln_g"], p["ln_b"], nb)
    new_conv = jnp.concatenate([conv_buf[:, 1:], hs[:, None, C_CONV:C_CONV + CONV_DIM]], 1)
    return (y.reshape(nb, 1, D_MODEL), y16, (new_conv, sg_new, sr_new, hl_new),
            (kb.reshape(nb, 1, N_HEADS, HEAD_DIM), v_self.reshape(nb, 1, N_HEADS, HEAD_DIM), ki[:, None, :IDX_DIM]))


def kernel(x_prompt, x_sample, cache_conv, cache_dsa_k, cache_dsa_v, cache_dsa_idx_k, cache_mem_k, cache_mem_v,
           state_gdn, state_ret, state_lru, page_table, mem_prompt, w_in, conv_w, conv_b, gdn_a_log, gdn_dt_bias,
           gdn_norm_w, lru_lambda, lru_w_r, lru_b_r, lru_w_i, lru_b_i, w_mem_kv, w_branch, w_out, ln_g, ln_b):
    bsz, t, _ = x_prompt.shape
    nb = x_sample.shape[0]
    depth = w_in.shape[0]
    past = page_table.shape[1] * PAGE_SIZE
    tabs_p = _rope_tables(np.arange(t))
    tabs_s = _rope_tables(np.asarray([past]))
    mem16 = _b(mem_prompt.reshape(-1, D_MODEL))
    xp, xs = x_prompt, x_sample
    xp16, xs16 = _b(xp.reshape(-1, D_MODEL)), _b(xs.reshape(-1, D_MODEL))
    acc_p, acc_s = [], []
    for l in range(depth):
        p = _layer_params(l, w_in, conv_w, conv_b, gdn_a_log, gdn_dt_bias, gdn_norm_w, lru_lambda, lru_w_r, lru_b_r,
                          lru_w_i, lru_b_i, w_mem_kv, w_branch, w_out, ln_g, ln_b)
        xp, xp16, st, rows, mk, mv = _prompt_layer(xp, xp16, mem16, p, tabs_p)
        acc_p.append(st + rows + (mk, mv))
        xs, xs16, st, rows = _sample_layer(xs, xs16, p, tabs_s, cache_conv[l], state_gdn[l], state_ret[l], state_lru[l],
                                           cache_mem_k[l], cache_mem_v[l], cache_dsa_k, cache_dsa_v,
                                           cache_dsa_idx_k, page_table, l)
        acc_s.append(st + rows)
    stack = lambda acc, i: jnp.stack([a[i] for a in acc])
    return (xp, xs) + tuple(stack(acc_p, i) for i in range(9)) + tuple(stack(acc_s, i) for i in range(7))
```
